```python
import jax, jax.numpy as jnp
from jax import lax
import numpy as np

D_MODEL = 1024
BATCH = 4
SEQ = 8192
DEPTH = 4

HEAD_DIM = 64
Q_BLOCK = 128
MLA_HEADS = 4
MLA_Q_RANK = 256
MLA_KV_RANK = 128
MLA_NOPE = 64
MLA_ROPE = 32
MLA_QK = MLA_NOPE + MLA_ROPE
MLA_V = 64
ROPE_THETA = 10000.0
SWA_Q_HEADS = 8
SWA_KV_HEADS = 2
SWA_WINDOW = 128
SB_HEADS = 4

WIDTH_A = MLA_HEADS * MLA_V
WIDTH_B = SWA_Q_HEADS * HEAD_DIM
WIDTH_C = SB_HEADS * HEAD_DIM
MIX_WIDTH = WIDTH_A + WIDTH_B + WIDTH_C

COLS_A = MLA_Q_RANK + MLA_KV_RANK + MLA_ROPE
COLS_B = (SWA_Q_HEADS + 2 * SWA_KV_HEADS) * HEAD_DIM
COLS_C = 3 * SB_HEADS * HEAD_DIM
IN_COLS = COLS_A + COLS_B + COLS_C

D_FF = 2816
N_EXPERTS = 8
TOP_K = 2
D_FF_EXPERT = 3584
MOE_BLOCK = 256
N_DENSE = (DEPTH + 1) // 2
N_MOE = DEPTH // 2

RMS_EPS = 1e-6
NEG = -1e30

kernel_name = "hybrid_mla_swa_stickbreaking_moe"


def rms_norm(x, g):
    xf = x.astype(jnp.float32)
    y = xf * lax.rsqrt(jnp.mean(xf * xf, axis=-1, keepdims=True) + RMS_EPS)
    return (y * g.astype(jnp.float32)).astype(x.dtype)


def rope(x, pos):
    half = x.shape[-1] // 2
    inv = ROPE_THETA ** (-jnp.arange(half, dtype=jnp.float32) / half)
    ang = pos.astype(jnp.float32)[:, None] * inv[None, :]
    cos = jnp.cos(ang)[None, :, None, :]
    sin = jnp.sin(ang)[None, :, None, :]
    xf = x.astype(jnp.float32)
    x1, x2 = xf[..., :half], xf[..., half:]
    return jnp.concatenate([x1 * cos - x2 * sin, x1 * sin + x2 * cos], axis=-1).astype(x.dtype)


def alibi_slopes(n):
    return jnp.exp2(-8.0 * jnp.arange(1, n + 1, dtype=jnp.float32) / n)


def to_query_blocks(q):
    B, S, H, D = q.shape
    return q.reshape(B, S // Q_BLOCK, Q_BLOCK, H, D).transpose(1, 0, 3, 2, 4)


def from_query_blocks(o):
    nb, B, QB, H, D = o.shape
    return o.transpose(1, 0, 2, 3, 4).reshape(B, nb * QB, H * D)


def dense_causal_softmax(q, k, v, scale):
    S = q.shape[1]
    key_pos = jnp.arange(S)

    def one_block(args):
        qi, bi = args
        s = jnp.einsum('bhqd,bkhd->bhqk', qi, k, preferred_element_type=jnp.float32) * scale
        q_pos = bi * Q_BLOCK + jnp.arange(Q_BLOCK)
        s = jnp.where(key_pos[None, :] <= q_pos[:, None], s, NEG)
        p = jax.nn.softmax(s, axis=-1).astype(v.dtype)
        return jnp.einsum('bhqk,bkhd->bqhd', p, v)

    o = lax.map(one_block, (to_query_blocks(q), jnp.arange(S // Q_BLOCK)))
    return from_query_blocks(o)


def mla_mixer(cols, pos, cq_norm_g, w_uq, ckv_norm_g, w_ukv, qn_g, kn_g):
    B, S, _ = cols.shape
    c_q = cols[..., :MLA_Q_RANK]
    c_kv = cols[..., MLA_Q_RANK:MLA_Q_RANK + MLA_KV_RANK]
    k_pe = cols[..., MLA_Q_RANK + MLA_KV_RANK:]
    q = (rms_norm(c_q, cq_norm_g) @ w_uq).reshape(B, S, MLA_HEADS, MLA_QK)
    kv = (rms_norm(c_kv, ckv_norm_g) @ w_ukv).reshape(B, S, MLA_HEADS, MLA_NOPE + MLA_V)
    k_nope, v = kv[..., :MLA_NOPE], kv[..., MLA_NOPE:]
    k = jnp.concatenate([k_nope, jnp.broadcast_to(k_pe[:, :, None, :], (B, S, MLA_HEADS, MLA_ROPE))], axis=-1)
    q = rms_norm(q, qn_g)
    k = rms_norm(k, kn_g)
    q = jnp.concatenate([q[..., :MLA_NOPE], rope(q[..., MLA_NOPE:], pos)], axis=-1)
    k = jnp.concatenate([k[..., :MLA_NOPE], rope(k[..., MLA_NOPE:], pos)], axis=-1)
    return dense_causal_softmax(q, k, v, MLA_QK ** -0.5)


def swa_sink_mixer(cols, qn_g, kn_g, sinks):
    B, S, _ = cols.shape
    W, Hq, Hkv, D = SWA_WINDOW, SWA_Q_HEADS, SWA_KV_HEADS, HEAD_DIM
    G = Hq // Hkv
    nb = S // W
    q = rms_norm(cols[..., :Hq * D].reshape(B, S, Hq, D), qn_g)
    k = rms_norm(cols[..., Hq * D:(Hq + Hkv) * D].reshape(B, S, Hkv, D), kn_g)
    v = cols[..., (Hq + Hkv) * D:].reshape(B, S, Hkv, D)

    def band(t):
        tp = jnp.pad(t, ((0, 0), (W, 0), (0, 0), (0, 0))).reshape(B, nb + 1, W, Hkv, D)
        return jnp.concatenate([tp[:, :-1], tp[:, 1:]], axis=2)

    kb, vb = band(k), band(v)
    qb = q.reshape(B, nb, W, Hkv, G, D)
    s = jnp.einsum('bnqhgd,bnkhd->bnhgqk', qb, kb, preferred_element_type=jnp.float32) * (D ** -0.5)
    dist = jnp.arange(W)[:, None] + W - jnp.arange(2 * W)[None, :]
    blk = jnp.arange(nb)[:, None, None]
    valid = (dist >= 0) & (dist < W) & (blk * W + jnp.arange(2 * W)[None, None, :] - W >= 0)
    slopes = alibi_slopes(Hq).reshape(Hkv, G)
    s = s - slopes[None, None, :, :, None, None] * dist.astype(jnp.float32)
    s = jnp.where(valid[None, :, None, None], s, NEG)
    sink = sinks.astype(jnp.float32).reshape(1, 1, Hkv, G, 1, 1)
    m = jnp.maximum(jnp.max(s, axis=-1, keepdims=True), sink)
    p = jnp.exp(s - m)
    probs = p / (jnp.sum(p, axis=-1, keepdims=True) + jnp.exp(sink - m))
    o = jnp.einsum('bnhgqk,bnkhd->bnqhgd', probs.astype(v.dtype), vb)
    return o.reshape(B, S, Hq * D)


def stick_breaking_mixer(cols):
    B, S, _ = cols.shape
    H, D = SB_HEADS, HEAD_DIM
    q = cols[..., :H * D].reshape(B, S, H, D)
    k = cols[..., H * D:2 * H * D].reshape(B, S, H, D)
    v = cols[..., 2 * H * D:].reshape(B, S, H, D)
    key_pos = jnp.arange(S)
    scale = D ** -0.5

    def one_block(args):
        qi, bi = args
        z = jnp.einsum('bhqd,bkhd->bhqk', qi, k, preferred_element_type=jnp.float32) * scale
        q_pos = bi * Q_BLOCK + jnp.arange(Q_BLOCK)
        mask = key_pos[None, :] < q_pos[:, None]
        log_1m_beta = jnp.where(mask, jax.nn.log_sigmoid(-z), 0.0)
        between = lax.cumsum(log_1m_beta, axis=3, reverse=True) - log_1m_beta
        a = jnp.where(mask, jnp.exp(jax.nn.log_sigmoid(z) + between), 0.0).astype(v.dtype)
        return jnp.einsum('bhqk,bkhd->bqhd', a, v)

    o = lax.map(one_block, (to_query_blocks(q), jnp.arange(S // Q_BLOCK)))
    return from_query_blocks(o)


def swiglu(h, w_gate, w_up, w_down):
    return (jax.nn.silu(h @ w_gate) * (h @ w_up)) @ w_down


def moe_swiglu(h, router_w, w_gate, w_up, w_down):
    B, S, D = h.shape
    N = B * S
    xf = h.reshape(N, D)
    logits = (xf @ router_w).astype(jnp.float32)
    top_logits, top_idx = lax.top_k(logits, TOP_K)
    gates = jax.nn.softmax(top_logits, axis=-1).astype(h.dtype)
    flat_e = top_idx.reshape(-1)
    flat_tok = jnp.repeat(jnp.arange(N, dtype=jnp.int32), TOP_K)
    flat_g = gates.reshape(-1)
    order = jnp.argsort(flat_e)
    sorted_e = flat_e[order]
    counts = jnp.bincount(flat_e, length=N_EXPERTS)
    padded = (counts + MOE_BLOCK - 1) // MOE_BLOCK * MOE_BLOCK
    start = jnp.cumsum(counts) - counts
    pend = jnp.cumsum(padded)
    pstart = pend - padded
    dest = pstart[sorted_e] + (jnp.arange(N * TOP_K) - start[sorted_e])
    P = N * TOP_K + N_EXPERTS * MOE_BLOCK
    n_blk = P // MOE_BLOCK
    slot_tok = jnp.zeros((P,), jnp.int32).at[dest].set(flat_tok[order])
    slot_gate = jnp.zeros((P,), h.dtype).at[dest].set(flat_g[order])
    blk_e = jnp.minimum(jnp.searchsorted(pend, jnp.arange(n_blk) * MOE_BLOCK, side='right'), N_EXPERTS - 1)
    xs = xf[slot_tok].reshape(n_blk, MOE_BLOCK, D)

    def expert_block(args):
        xb, e = args
        return swiglu(xb, w_gate[e], w_up[e], w_down[e])

    ys = lax.map(expert_block, (xs, blk_e)).reshape(P, D) * slot_gate[:, None]
    return jnp.zeros((N, D), h.dtype).at[slot_tok].add(ys).reshape(B, S, D)


def setup_inputs(seed: int = 0) -> dict:
    keys = jax.random.split(jax.random.key(seed), 22)

    def nrm(i, shape, scale):
        return jax.random.normal(keys[i], shape, jnp.float32) * scale

    def gain(i, shape):
        return 1.0 + nrm(i, shape, 0.02)

    L = DEPTH
    return {
        "x": nrm(0, (BATCH, SEQ, D_MODEL), 1.0),
        "attn_norm_g": gain(1, (L, D_MODEL)),
        "w_in": nrm(2, (L, D_MODEL, IN_COLS), D_MODEL ** -0.5),
        "mla_cq_norm_g": gain(3, (L, MLA_Q_RANK)),
        "mla_w_uq": nrm(4, (L, MLA_Q_RANK, MLA_HEADS * MLA_QK), MLA_Q_RANK ** -0.5),
        "mla_ckv_norm_g": gain(5, (L, MLA_KV_RANK)),
        "mla_w_ukv": nrm(6, (L, MLA_KV_RANK, MLA_HEADS * (MLA_NOPE + MLA_V)), MLA_KV_RANK ** -0.5),
        "mla_qn_g": gain(7, (L, MLA_QK)),
        "mla_kn_g": gain(8, (L, MLA_QK)),
        "swa_qn_g": gain(9, (L, HEAD_DIM)),
        "swa_kn_g": gain(10, (L, HEAD_DIM)),
        "swa_sinks": nrm(11, (L, SWA_Q_HEADS), 1.0),
        "out_norm_g": gain(12, (L, MIX_WIDTH)),
        "w_out": nrm(13, (L, MIX_WIDTH, D_MODEL), MIX_WIDTH ** -0.5),
        "ffn_norm_g": gain(14, (L, D_MODEL)),
        "dense_w_gate": nrm(15, (N_DENSE, D_MODEL, D_FF), D_MODEL ** -0.5),
        "dense_w_up": nrm(16, (N_DENSE, D_MODEL, D_FF), D_MODEL ** -0.5),
        "dense_w_down": nrm(17, (N_DENSE, D_FF, D_MODEL), D_FF ** -0.5),
        "router_w": nrm(18, (N_MOE, D_MODEL, N_EXPERTS), D_MODEL ** -0.5),
        "moe_w_gate": nrm(19, (N_MOE, N_EXPERTS, D_MODEL, D_FF_EXPERT), D_MODEL ** -0.5),
        "moe_w_up": nrm(20, (N_MOE, N_EXPERTS, D_MODEL, D_FF_EXPERT), D_MODEL ** -0.5),
        "moe_w_down": nrm(21, (N_MOE, N_EXPERTS, D_FF_EXPERT, D_MODEL), D_FF_EXPERT ** -0.5),
    }


def reference(x, attn_norm_g, w_in, mla_cq_norm_g, mla_w_uq, mla_ckv_norm_g, mla_w_ukv,
              mla_qn_g, mla_kn_g, swa_qn_g, swa_kn_g, swa_sinks, out_norm_g, w_out,
              ffn_norm_g, dense_w_gate, dense_w_up, dense_w_down, router_w,
              moe_w_gate, moe_w_up, moe_w_down):
    pos = jnp.arange(x.shape[1])
    for layer in range(DEPTH):
        h = rms_norm(x, attn_norm_g[layer])
        cols = h @ w_in[layer]
        o_a = mla_mixer(cols[..., :COLS_A], pos, mla_cq_norm_g[layer], mla_w_uq[layer],
                        mla_ckv_norm_g[layer], mla_w_ukv[layer], mla_qn_g[layer], mla_kn_g[layer])
        o_b = swa_sink_mixer(cols[..., COLS_A:COLS_A + COLS_B], swa_qn_g[layer], swa_kn_g[layer],
                             swa_sinks[layer])
        o_c = stick_breaking_mixer(cols[..., COLS_A + COLS_B:])
        g = out_norm_g[layer]
        mix = jnp.concatenate([
            rms_norm(o_a, g[:WIDTH_A]),
            rms_norm(o_b, g[WIDTH_A:WIDTH_A + WIDTH_B]),
            rms_norm(o_c, g[WIDTH_A + WIDTH_B:]),
        ], axis=-1)
        x = x + mix @ w_out[layer]
        h = rms_norm(x, ffn_norm_g[layer])
        j = layer // 2
        if layer % 2 == 0:
            x = x + swiglu(h, dense_w_gate[j], dense_w_up[j], dense_w_down[j])
        else:
            x = x + moe_swiglu(h, router_w[j], moe_w_gate[j], moe_w_up[j], moe_w_down[j])
    return x
```

```python
import functools

import jax
import jax.numpy as jnp
from jax import lax
from jax.experimental import pallas as pl
from jax.experimental.pallas import tpu as pltpu

F32 = jnp.float32
BF16 = jnp.bfloat16

D_MODEL = 1024
HEAD_DIM = 64
MLA_HEADS = 4
MLA_Q_RANK = 256
MLA_KV_RANK = 128
MLA_NOPE = 64
MLA_ROPE = 32
MLA_QK = MLA_NOPE + MLA_ROPE
MLA_V = 64
ROPE_THETA = 10000.0
SWA_Q_HEADS = 8
SWA_KV_HEADS = 2
SWA_WINDOW = 128
SB_HEADS = 4
WIDTH_A = MLA_HEADS * MLA_V
WIDTH_B = SWA_Q_HEADS * HEAD_DIM
WIDTH_C = SB_HEADS * HEAD_DIM
COLS_A = MLA_Q_RANK + MLA_KV_RANK + MLA_ROPE
COLS_B = (SWA_Q_HEADS + 2 * SWA_KV_HEADS) * HEAD_DIM
D_FF = 2816
N_EXPERTS = 8
D_FF_EXPERT = 3584
RMS_EPS = 1e-6
NEG = -1e30

LANES = 128
VMEM_LIMIT = 56 * 1024 * 1024

TM_PROJ = 512
TQ_ATTN = 256
TQ_SWA = 512
TM_FFN = 512
TF_FFN = 1408
TM_MOE = 512
TF_MOE = 1792

P_CQ = 0
P_CKV = 256
P_KPE = 384
P_SWQ = 512
P_SWK = 1024
P_SWV = 1280
P_SBQ = 1536
P_SBK = 1792
P_SBV = 2048
P_COLS = 2304


def _dot(a, b):
    return jnp.dot(a, b, preferred_element_type=F32)


def _dot_nt(a, b):
    return lax.dot_general(a, b, (((1,), (1,)), ((), ())), preferred_element_type=F32)


def _rms(x, g):
    return x * lax.rsqrt(jnp.mean(x * x, axis=-1, keepdims=True) + RMS_EPS) * g


def _in_proj_kernel(x_ref, g_ref, w_ref, gcq_ref, wuq_ref, gckv_ref, wukv_ref, gq_ref, gk_ref,
                    cos_ref, sina_ref, sinb_ref, sgq_ref, sgk_ref,
                    mq_ref, mk_ref, mv_ref, sq_ref, sk_ref, sv_ref, cq_ref, ck_ref, cv_ref):
    h = _rms(x_ref[...], g_ref[...]).astype(BF16)

    def proj(lo, hi):
        return _dot(h, w_ref[:, lo:hi])

    cos = cos_ref[...]
    sina = sina_ref[...]
    sinb = sinb_ref[...]

    def rope(t):
        return t * cos + pltpu.roll(t, LANES - MLA_ROPE // 2, 1) * sina + pltpu.roll(t, MLA_ROPE // 2, 1) * sinb

    def head_norm(t, g):
        return t * lax.rsqrt(jnp.sum(t * t, axis=-1, keepdims=True) * (1.0 / MLA_QK) + RMS_EPS) * g

    a = proj(P_CQ, P_SWQ)
    q = _dot(_rms(a[:, :MLA_Q_RANK], gcq_ref[...]).astype(BF16), wuq_ref[...])
    for hh in range(MLA_HEADS):
        sl = slice(LANES * hh, LANES * (hh + 1))
        mq_ref[:, sl] = rope(head_norm(q[:, sl], gq_ref[...])).astype(BF16)
    kv = _dot(_rms(a[:, P_CKV:P_KPE], gckv_ref[...]).astype(BF16), wukv_ref[...])
    kpe = a[:, P_KPE:P_SWQ]
    for hh in range(MLA_HEADS):
        sl = slice(LANES * hh, LANES * (hh + 1))
        mk_ref[:, sl] = rope(head_norm(kv[:, sl] + kpe, gk_ref[...])).astype(BF16)
    mv_ref[...] = kv[:, MLA_HEADS * LANES:].astype(BF16)

    lane = lax.broadcasted_iota(jnp.int32, (1, LANES), 1)
    low = lane < HEAD_DIM

    def pair_norm(t, g):
        t2 = t * t
        lo = jnp.sum(jnp.where(low, t2, 0.0), axis=-1, keepdims=True)
        hi = jnp.sum(jnp.where(low, 0.0, t2), axis=-1, keepdims=True)
        ms = jnp.where(low, lo, hi) * (1.0 / HEAD_DIM)
        return t * lax.rsqrt(ms + RMS_EPS) * g

    scale = HEAD_DIM ** -0.5
    bq = proj(P_SWQ, P_SWK)
    for gi in range(WIDTH_B // LANES):
        sl = slice(LANES * gi, LANES * (gi + 1))
        sq_ref[:, sl] = (pair_norm(bq[:, sl], sgq_ref[...]) * scale).astype(BF16)
    bk = proj(P_SWK, P_SWV)
    for gi in range(SWA_KV_HEADS):
        sl = slice(LANES * gi, LANES * (gi + 1))
        sk_ref[:, sl] = pair_norm(bk[:, sl], sgk_ref[...]).astype(BF16)
    sv_ref[...] = proj(P_SWV, P_SBQ).astype(BF16)
    cq_ref[...] = (proj(P_SBQ, P_SBK) * scale).astype(BF16)
    ck_ref[...] = proj(P_SBK, P_SBV).astype(BF16)
    cv_ref[...] = proj(P_SBV, P_COLS).astype(BF16)


def _in_proj(x, lw, tabs, seq):
    n = x.shape[0]
    tm = TM_PROJ
    nt = seq // tm
    row = lambda i: (i, 0)
    fixed = lambda i: (0, 0)
    pos = lambda i: (i % nt, 0)
    full = lambda a: pl.BlockSpec(a.shape, fixed)
    cos, sina, sinb = tabs
    widths = (4 * LANES, 4 * LANES, WIDTH_A, WIDTH_B, 2 * LANES, 2 * LANES, WIDTH_C, WIDTH_C, WIDTH_C)
    return pl.pallas_call(
        _in_proj_kernel,
        grid=(n // tm,),
        in_specs=[pl.BlockSpec((tm, D_MODEL), row), full(lw["attn_g"]), full(lw["w_in"]), full(lw["cq_g"]),
                  full(lw["w_uq"]), full(lw["ckv_g"]), full(lw["w_ukv"]), full(lw["mla_qg"]), full(lw["mla_kg"]),
                  pl.BlockSpec((tm, LANES), pos), pl.BlockSpec((tm, LANES), pos), pl.BlockSpec((tm, LANES), pos),
                  full(lw["swa_qg"]), full(lw["swa_kg"])],
        out_specs=[pl.BlockSpec((tm, w), row) for w in widths],
        out_shape=[jax.ShapeDtypeStruct((n, w), BF16) for w in widths],
        compiler_params=pltpu.CompilerParams(dimension_semantics=("parallel",), vmem_limit_bytes=VMEM_LIMIT),
        name="in_proj",
    )(x, lw["attn_g"], lw["w_in"], lw["cq_g"], lw["w_uq"], lw["ckv_g"], lw["w_ukv"], lw["mla_qg"], lw["mla_kg"],
      cos, sina, sinb, lw["swa_qg"], lw["swa_kg"])


def _mla_kernel(q_ref, k_ref, v_ref, o_ref, *, tq):
    i = pl.program_id(2)
    q = q_ref[...]
    scale = MLA_QK ** -0.5
    row = lax.broadcasted_iota(jnp.int32, (tq, tq), 0)
    col = lax.broadcasted_iota(jnp.int32, (tq, tq), 1)

    def step(j, carry, masked):
        start = pl.multiple_of(j * tq, tq)
        k = k_ref[pl.ds(start, tq), :]
        v = v_ref[pl.ds(start, tq), :]
        new = []
        for hh in range(2):
            m, l, acc = carry[hh]
            sl = slice(LANES * hh, LANES * (hh + 1))
            s = _dot_nt(q[:, sl], k[:, sl]) * scale
            if masked:
                s = jnp.where(col <= row, s, NEG)
            m_new = jnp.maximum(m, jnp.max(s, axis=-1, keepdims=True))
            alpha = jnp.exp(m - m_new)
            p = jnp.exp(s - m_new)
            l = alpha * l + jnp.sum(p, axis=-1, keepdims=True)
            acc = alpha * acc + _dot(p.astype(BF16), v)
            new.append((m_new, l, acc))
        return tuple(new)

    init = tuple((jnp.full((tq, 1), NEG, F32), jnp.zeros((tq, 1), F32), jnp.zeros((tq, LANES), F32))
                 for _ in range(2))
    carry = lax.fori_loop(0, i, lambda j, c: step(j, c, False), init)
    (_, l0, a0), (_, l1, a1) = step(i, carry, True)
    lane = lax.broadcasted_iota(jnp.int32, (1, LANES), 1)
    o_ref[...] = jnp.where(lane < MLA_V, a0 / l0, a1 / l1)


def _mla_attention(q, k, v, batch, seq):
    tq = TQ_ATTN
    q = q.reshape(batch, seq, -1)
    k = k.reshape(batch, seq, -1)
    v = v.reshape(batch, seq, -1)
    out = pl.pallas_call(
        functools.partial(_mla_kernel, tq=tq),
        grid=(batch, MLA_HEADS // 2, seq // tq),
        in_specs=[pl.BlockSpec((None, tq, 2 * LANES), lambda b, p, i: (b, i, p)),
                  pl.BlockSpec((None, seq, 2 * LANES), lambda b, p, i: (b, 0, p)),
                  pl.BlockSpec((None, seq, LANES), lambda b, p, i: (b, 0, p))],
        out_specs=pl.BlockSpec((None, tq, LANES), lambda b, p, i: (b, i, p)),
        out_shape=jax.ShapeDtypeStruct((batch, seq, WIDTH_A), F32),
        compiler_params=pltpu.CompilerParams(dimension_semantics=("parallel", "parallel", "arbitrary"),
                                             vmem_limit_bytes=VMEM_LIMIT),
        name="mla_attention",
    )(q, k, v)
    return out.reshape(batch * seq, WIDTH_A)


def _sb_kernel(q_ref, k_ref, v_ref, tri_ref, o_ref, *, tq):
    i = pl.program_id(2)
    q = q_ref[...]
    lane = lax.broadcasted_iota(jnp.int32, (1, LANES), 1)
    low = lane < HEAD_DIM
    zero = jnp.zeros_like(q)
    qh = (jnp.where(low, q, zero), jnp.where(low, zero, q))
    row = lax.broadcasted_iota(jnp.int32, (tq, tq), 0)
    col = lax.broadcasted_iota(jnp.int32, (tq, tq), 1)
    strict = col < row
    tri = tri_ref[...]

    def step(j, carry, masked):
        start = pl.multiple_of(j * tq, tq)
        k = k_ref[pl.ds(start, tq), :]
        v = v_ref[pl.ds(start, tq), :]
        new = []
        for hh in range(2):
            c, acc = carry[hh]
            z = _dot_nt(qh[hh], k)
            lg = jnp.minimum(-z, 0.0) - jnp.log(1.0 + jnp.exp(-jnp.abs(z)))
            if masked:
                lg = jnp.where(strict, lg, 0.0)
            hi = lg.astype(BF16)
            lo = (lg - hi.astype(F32)).astype(BF16)
            between = _dot(hi, tri) + _dot(lo, tri) + c
            a = jnp.exp(z + lg + between)
            if masked:
                a = jnp.where(strict, a, 0.0)
            acc = acc + _dot(a.astype(BF16), v)
            c = c + jnp.sum(lg, axis=-1, keepdims=True)
            new.append((c, acc))
        return tuple(new)

    init = tuple((jnp.zeros((tq, 1), F32), jnp.zeros((tq, LANES), F32)) for _ in range(2))
    carry = step(i, init, True)
    (_, a0), (_, a1) = lax.fori_loop(0, i, lambda t, c: step(i - 1 - t, c, False), carry)
    o_ref[...] = jnp.where(low, a0, a1)


def _sb_attention(q, k, v, tri, batch, seq):
    tq = TQ_ATTN
    q = q.reshape(batch, seq, -1)
    k = k.reshape(batch, seq, -1)
    v = v.reshape(batch, seq, -1)
    out = pl.pallas_call(
        functools.partial(_sb_kernel, tq=tq),
        grid=(batch, SB_HEADS // 2, seq // tq),
        in_specs=[pl.BlockSpec((None, tq, LANES), lambda b, p, i: (b, i, p)),
                  pl.BlockSpec((None, seq, LANES), lambda b, p, i: (b, 0, p)),
                  pl.BlockSpec((None, seq, LANES), lambda b, p, i: (b, 0, p)),
                  pl.BlockSpec((tq, tq), lambda b, p, i: (0, 0))],
        out_specs=pl.BlockSpec((None, tq, LANES), lambda b, p, i: (b, i, p)),
        out_shape=jax.ShapeDtypeStruct((batch, seq, WIDTH_C), F32),
        compiler_params=pltpu.CompilerParams(dimension_semantics=("parallel", "parallel", "arbitrary"),
                                             vmem_limit_bytes=VMEM_LIMIT),
        name="sb_attention",
    )(q, k, v, tri)
    return out.reshape(batch * seq, WIDTH_C)


def _swa_kernel(slopes_ref, sinks_ref, q_ref, k_ref, v_ref, o_ref, *, tq):
    w = SWA_WINDOW
    group = SWA_Q_HEADS // SWA_KV_HEADS
    hkv = pl.program_id(1)
    i = pl.program_id(2)
    lane = lax.broadcasted_iota(jnp.int32, (1, LANES), 1)
    low = lane < HEAD_DIM
    row = lax.broadcasted_iota(jnp.int32, (w, 2 * w), 0)
    col = lax.broadcasted_iota(jnp.int32, (w, 2 * w), 1)
    for n in range(tq // w):
        q0 = i * tq + n * w
        ks = pl.multiple_of(jnp.maximum(q0 - w, 0), w)
        k = k_ref[pl.ds(ks, 2 * w), :]
        v = v_ref[pl.ds(ks, 2 * w), :]
        dist = (q0 - ks) + row - col
        valid = (dist >= 0) & (dist < w)
        distf = dist.astype(F32)
        for pr in range(group // 2):
            qp = q_ref[n * w:(n + 1) * w, LANES * pr:LANES * (pr + 1)]
            zero = jnp.zeros_like(qp)
            outs = []
            for half in range(2):
                head = hkv * group + pr * 2 + half
                qh = jnp.where(low, qp, zero) if half == 0 else jnp.where(low, zero, qp)
                s = _dot_nt(qh, k) - slopes_ref[head] * distf
                s = jnp.where(valid, s, NEG)
                sink = sinks_ref[head]
                m = jnp.maximum(jnp.max(s, axis=-1, keepdims=True), sink)
                p = jnp.exp(s - m)
                den = jnp.sum(p, axis=-1, keepdims=True) + jnp.exp(sink - m)
                outs.append(_dot(p.astype(BF16), v) / den)
            o_ref[n * w:(n + 1) * w, LANES * pr:LANES * (pr + 1)] = jnp.where(low, outs[0], outs[1])


def _swa_attention(q, k, v, slopes, sinks, batch, seq):
    tq = TQ_SWA
    q = q.reshape(batch, seq, -1)
    k = k.reshape(batch, seq, -1)
    v = v.reshape(batch, seq, -1)
    smem = pl.BlockSpec(memory_space=pltpu.SMEM)
    out = pl.pallas_call(
        functools.partial(_swa_kernel, tq=tq),
        grid=(batch, SWA_KV_HEADS, seq // tq),
        in_specs=[smem, smem,
                  pl.BlockSpec((None, tq, 2 * LANES), lambda b, h, i: (b, i, h)),
                  pl.BlockSpec((None, seq, LANES), lambda b, h, i: (b, 0, h)),
                  pl.BlockSpec((None, seq, LANES), lambda b, h, i: (b, 0, h))],
        out_specs=pl.BlockSpec((None, tq, 2 * LANES), lambda b, h, i: (b, i, h)),
        out_shape=jax.ShapeDtypeStruct((batch, seq, WIDTH_B), F32),
        compiler_params=pltpu.CompilerParams(dimension_semantics=("parallel", "parallel", "arbitrary"),
                                             vmem_limit_bytes=VMEM_LIMIT),
        name="swa_attention",
    )(slopes, sinks, q, k, v)
    return out.reshape(batch * seq, WIDTH_B)


def _out_proj_kernel(*refs, moe):
    if moe:
        oa, ob, oc, x_ref, gout, wout, gffn, wr_hi, wr_lo, xo_ref, h_ref, sel_ref = refs
    else:
        oa, ob, oc, x_ref, gout, wout, gffn, xo_ref, h_ref = refs
    g = gout[...]
    b0, b1 = WIDTH_A, WIDTH_A + WIDTH_B
    na = _rms(oa[...], g[:, :b0]).astype(BF16)
    nb = _rms(ob[...], g[:, b0:b1]).astype(BF16)
    nc = _rms(oc[...], g[:, b1:]).astype(BF16)
    xn = x_ref[...] + _dot(na, wout[:b0, :]) + _dot(nb, wout[b0:b1, :]) + _dot(nc, wout[b1:, :])
    xo_ref[...] = xn
    hf = _rms(xn, gffn[...])
    hb = hf.astype(BF16)
    h_ref[...] = hb
    if moe:
        lo = (hf - hb.astype(F32)).astype(BF16)
        logits = _dot(hb, wr_hi[...]) + _dot(lo, wr_hi[...]) + _dot(hb, wr_lo[...])
        lane = lax.broadcasted_iota(jnp.int32, logits.shape, 1)
        lg = jnp.where(lane < N_EXPERTS, logits, -jnp.inf)
        m1 = jnp.max(lg, axis=-1, keepdims=True)
        i1 = jnp.min(jnp.where(lg == m1, lane, LANES), axis=-1, keepdims=True)
        lg2 = jnp.where(lane == i1, -jnp.inf, lg)
        m2 = jnp.max(lg2, axis=-1, keepdims=True)
        i2 = jnp.min(jnp.where(lg2 == m2, lane, LANES), axis=-1, keepdims=True)
        e = jnp.exp(m2 - m1)
        g1 = 1.0 / (1.0 + e)
        g2 = e / (1.0 + e)
        sel_ref[...] = jnp.where(lane == 0, i1.astype(F32),
                                 jnp.where(lane == 1, i2.astype(F32),
                                           jnp.where(lane == 2, g1, jnp.where(lane == 3, g2, 0.0))))


def _out_proj(oa, ob, oc, x, lw, moe):
    n = x.shape[0]
    tm = TM_PROJ
    row = lambda i: (i, 0)
    fixed = lambda i: (0, 0)
    full = lambda a: pl.BlockSpec(a.shape, fixed)
    args = [oa, ob, oc, x, lw["out_g"], lw["w_out"], lw["ffn_g"]]
    in_specs = [pl.BlockSpec((tm, WIDTH_A), row), pl.BlockSpec((tm, WIDTH_B), row), pl.BlockSpec((tm, WIDTH_C), row),
                pl.BlockSpec((tm, D_MODEL), row), full(lw["out_g"]), full(lw["w_out"]), full(lw["ffn_g"])]
    out_specs = [pl.BlockSpec((tm, D_MODEL), row), pl.BlockSpec((tm, D_MODEL), row)]
    out_shape = [jax.ShapeDtypeStruct((n, D_MODEL), F32), jax.ShapeDtypeStruct((n, D_MODEL), BF16)]
    if moe:
        args += [lw["router_hi"], lw["router_lo"]]
        in_specs += [full(lw["router_hi"]), full(lw["router_lo"])]
        out_specs.append(pl.BlockSpec((tm, LANES), row))
        out_shape.append(jax.ShapeDtypeStruct((n, LANES), F32))
    return pl.pallas_call(
        functools.partial(_out_proj_kernel, moe=moe),
        grid=(n // tm,),
        in_specs=in_specs, out_specs=out_specs, out_shape=out_shape,
        compiler_params=pltpu.CompilerParams(dimension_semantics=("parallel",), vmem_limit_bytes=VMEM_LIMIT),
        name="out_proj_moe" if moe else "out_proj",
    )(*args)


def _swiglu_partial(h, wg, wu, wd):
    g = _dot(h, wg)
    u = _dot(h, wu)
    return _dot((g * jax.nn.sigmoid(g) * u).astype(BF16), wd)


def _ffn_kernel(h_ref, wg_ref, wu_ref, wd_ref, x_ref, o_ref):
    j = pl.program_id(1)
    y = _swiglu_partial(h_ref[...], wg_ref[...], wu_ref[...], wd_ref[...])

    @pl.when(j == 0)
    def _():
        o_ref[...] = x_ref[...] + y

    @pl.when(j > 0)
    def _():
        o_ref[...] += y


def _dense_ffn(h, x, lw):
    n = x.shape[0]
    tm, tf = TM_FFN, TF_FFN
    return pl.pallas_call(
        _ffn_kernel,
        grid=(n // tm, D_FF // tf),
        in_specs=[pl.BlockSpec((tm, D_MODEL), lambda i, j: (i, 0)),
                  pl.BlockSpec((D_MODEL, tf), lambda i, j: (0, j)),
                  pl.BlockSpec((D_MODEL, tf), lambda i, j: (0, j)),
                  pl.BlockSpec((tf, D_MODEL), lambda i, j: (j, 0)),
                  pl.BlockSpec((tm, D_MODEL), lambda i, j: (i, 0))],
        out_specs=pl.BlockSpec((tm, D_MODEL), lambda i, j: (i, 0)),
        out_shape=jax.ShapeDtypeStruct((n, D_MODEL), F32),
        compiler_params=pltpu.CompilerParams(dimension_semantics=("parallel", "arbitrary"),
                                             vmem_limit_bytes=VMEM_LIMIT),
        name="dense_ffn",
    )(h, lw["w_gate"], lw["w_up"], lw["w_down"], x)


def _moe_kernel(blk_e_ref, n_used_ref, xs_ref, wg_ref, wu_ref, wd_ref, o_ref):
    i = pl.program_id(0)
    j = pl.program_id(1)
    used = i < n_used_ref[0]

    @pl.when(used)
    def _():
        y = _swiglu_partial(xs_ref[...], wg_ref[...], wu_ref[...], wd_ref[...])

        @pl.when(j == 0)
        def _():
            o_ref[...] = y

        @pl.when(j > 0)
        def _():
            o_ref[...] += y

    @pl.when(jnp.logical_not(used) & (j == 0))
    def _():
        o_ref[...] = jnp.zeros_like(o_ref)


def _moe_experts(xs, blk_e, n_used, lw):
    p = xs.shape[0]
    tm, tf = TM_MOE, TF_MOE
    n_ff = D_FF_EXPERT // tf

    def col(i, j):
        return jnp.where(i % 2 == 0, j, n_ff - 1 - j)

    grid_spec = pltpu.PrefetchScalarGridSpec(
        num_scalar_prefetch=2,
        grid=(p // tm, n_ff),
        in_specs=[pl.BlockSpec((tm, D_MODEL), lambda i, j, be, nu: (i, 0)),
                  pl.BlockSpec((None, D_MODEL, tf), lambda i, j, be, nu: (be[i], 0, col(i, j))),
                  pl.BlockSpec((None, D_MODEL, tf), lambda i, j, be, nu: (be[i], 0, col(i, j))),
                  pl.BlockSpec((None, tf, D_MODEL), lambda i, j, be, nu: (be[i], col(i, j), 0))],
        out_specs=pl.BlockSpec((tm, D_MODEL), lambda i, j, be, nu: (i, 0)),
    )
    return pl.pallas_call(
        _moe_kernel,
        grid_spec=grid_spec,
        out_shape=jax.ShapeDtypeStruct((p, D_MODEL), F32),
        compiler_params=pltpu.CompilerParams(dimension_semantics=("arbitrary", "arbitrary"),
                                             vmem_limit_bytes=VMEM_LIMIT),
        name="moe_experts",
    )(blk_e, n_used, xs, lw["w_gate"], lw["w_up"], lw["w_down"])


def _moe_ffn(h, x, sel, lw):
    n = x.shape[0]
    tm = TM_MOE
    top_idx = sel[:, :2].astype(jnp.int32)
    gates = sel[:, 2:4]
    flat_e = top_idx.reshape(-1)
    onehot = (flat_e[:, None] == jnp.arange(N_EXPERTS, dtype=jnp.int32)[None, :]).astype(jnp.int32)
    csum = jnp.cumsum(onehot, axis=0)
    rank = jnp.sum(csum * onehot, axis=1) - 1
    counts = csum[-1]
    padded = (counts + tm - 1) // tm * tm
    pend = jnp.cumsum(padded)
    pstart = pend - padded
    dest = pstart[flat_e] + rank
    p = n * 2 + N_EXPERTS * tm
    n_blk = p // tm
    slot_tok = jnp.zeros((p,), jnp.int32).at[dest].set(jnp.arange(2 * n, dtype=jnp.int32) // 2)
    blk_e = jnp.minimum(jnp.searchsorted(pend, jnp.arange(n_blk, dtype=jnp.int32) * tm, side="right"),
                        N_EXPERTS - 1).astype(jnp.int32)
    n_used = (pend[-1:] // tm).astype(jnp.int32)
    xs = h[slot_tok]
    ys = _moe_experts(xs, blk_e, n_used, lw)
    dest2 = dest.reshape(n, 2)
    return x + ys[dest2[:, 0]] * gates[:, 0:1] + ys[dest2[:, 1]] * gates[:, 1:2]


def _pad_cols(w, width):
    return jnp.pad(w, ((0, 0), (0, width - w.shape[1])))


def _head_groups(w, heads, dim):
    kdim = w.shape[0]
    return jnp.pad(w.reshape(kdim, heads, dim), ((0, 0), (0, 0), (0, LANES - dim))).reshape(kdim, heads * LANES)


def _layer_weights(layer, p):
    w_in = p["w_in"][layer]
    a, b, c = w_in[:, :COLS_A], w_in[:, COLS_A:COLS_A + COLS_B], w_in[:, COLS_A + COLS_B:]
    kpe = jnp.pad(a[:, MLA_Q_RANK + MLA_KV_RANK:], ((0, 0), (MLA_NOPE, LANES - MLA_QK)))
    bq = b[:, :WIDTH_B]

    def dup(w):
        return jnp.repeat(w.reshape(D_MODEL, SWA_KV_HEADS, 1, HEAD_DIM), 2, axis=2).reshape(D_MODEL, 2 * LANES)

    bk = dup(b[:, WIDTH_B:WIDTH_B + SWA_KV_HEADS * HEAD_DIM])
    bv = dup(b[:, WIDTH_B + SWA_KV_HEADS * HEAD_DIM:])
    w_in_p = jnp.concatenate([a[:, :MLA_Q_RANK + MLA_KV_RANK], kpe, bq, bk, bv, c], axis=1).astype(BF16)
    w_ukv = p["mla_w_ukv"][layer].reshape(MLA_KV_RANK, MLA_HEADS, MLA_NOPE + MLA_V)
    w_uk = _head_groups(w_ukv[:, :, :MLA_NOPE].reshape(MLA_KV_RANK, -1), MLA_HEADS, MLA_NOPE)
    w_uv = w_ukv[:, :, MLA_NOPE:].reshape(MLA_KV_RANK, -1)
    row = lambda g: g.reshape(1, -1).astype(F32)
    lw = {
        "attn_g": row(p["attn_norm_g"][layer]),
        "w_in": w_in_p,
        "cq_g": row(p["mla_cq_norm_g"][layer]),
        "w_uq": _head_groups(p["mla_w_uq"][layer], MLA_HEADS, MLA_QK).astype(BF16),
        "ckv_g": row(p["mla_ckv_norm_g"][layer]),
        "w_ukv": jnp.concatenate([w_uk, w_uv], axis=1).astype(BF16),
        "mla_qg": _pad_cols(row(p["mla_qn_g"][layer]), LANES),
        "mla_kg": _pad_cols(row(p["mla_kn_g"][layer]), LANES),
        "swa_qg": jnp.tile(row(p["swa_qn_g"][layer]), (1, 2)),
        "swa_kg": jnp.tile(row(p["swa_kn_g"][layer]), (1, 2)),
        "sinks": p["swa_sinks"][layer].astype(F32),
        "out_g": row(p["out_norm_g"][layer]),
        "w_out": p["w_out"][layer].astype(BF16),
        "ffn_g": row(p["ffn_norm_g"][layer]),
    }
    j = layer // 2
    if layer % 2 == 0:
        lw["w_gate"] = p["dense_w_gate"][j].astype(BF16)
        lw["w_up"] = p["dense_w_up"][j].astype(BF16)
        lw["w_down"] = p["dense_w_down"][j].astype(BF16)
    else:
        rw = _pad_cols(p["router_w"][j], LANES)
        hi = rw.astype(BF16)
        lw["router_hi"] = hi
        lw["router_lo"] = (rw - hi.astype(F32)).astype(BF16)
        lw["w_gate"] = p["moe_w_gate"][j].astype(BF16)
        lw["w_up"] = p["moe_w_up"][j].astype(BF16)
        lw["w_down"] = p["moe_w_down"][j].astype(BF16)
    return lw


def _rope_tables(seq):
    half = MLA_ROPE // 2
    inv = ROPE_THETA ** (-jnp.arange(half, dtype=F32) / half)
    ang = jnp.arange(seq).astype(F32)[:, None] * inv[None, :]
    cos, sin = jnp.cos(ang), jnp.sin(ang)
    zl = jnp.zeros((seq, MLA_NOPE), F32)
    zh = jnp.zeros((seq, half), F32)
    zr = jnp.zeros((seq, LANES - MLA_QK), F32)
    cos_t = jnp.concatenate([zl + 1.0, cos, cos, zr + 1.0], axis=1)
    sina = jnp.concatenate([zl, -sin, zh, zr], axis=1)
    sinb = jnp.concatenate([zl, zh, sin, zr], axis=1)
    return cos_t, sina, sinb


def kernel(x, attn_norm_g, w_in, mla_cq_norm_g, mla_w_uq, mla_ckv_norm_g, mla_w_ukv, mla_qn_g, mla_kn_g, swa_qn_g, swa_kn_g, swa_sinks, out_norm_g, w_out, ffn_norm_g, dense_w_gate, dense_w_up, dense_w_down, router_w, moe_w_gate, moe_w_up, moe_w_down):
    params = dict(attn_norm_g=attn_norm_g, w_in=w_in, mla_cq_norm_g=mla_cq_norm_g, mla_w_uq=mla_w_uq,
                  mla_ckv_norm_g=mla_ckv_norm_g, mla_w_ukv=mla_w_ukv, mla_qn_g=mla_qn_g, mla_kn_g=mla_kn_g,
                  swa_qn_g=swa_qn_g, swa_kn_g=swa_kn_g, swa_sinks=swa_sinks, out_norm_g=out_norm_g, w_out=w_out,
                  ffn_norm_g=ffn_norm_g, dense_w_gate=dense_w_gate, dense_w_up=dense_w_up,
                  dense_w_down=dense_w_down, router_w=router_w, moe_w_gate=moe_w_gate, moe_w_up=moe_w_up,
                  moe_w_down=moe_w_down)
    batch, seq, _ = x.shape
    depth = w_in.shape[0]
    tabs = _rope_tables(seq)
    slopes = jnp.exp2(-8.0 * jnp.arange(1, SWA_Q_HEADS + 1, dtype=F32) / SWA_Q_HEADS)
    tq = TQ_ATTN
    tri = (jnp.arange(tq)[:, None] > jnp.arange(tq)[None, :]).astype(BF16)
    xf = x.reshape(batch * seq, D_MODEL)
    for layer in range(depth):
        lw = _layer_weights(layer, params)
        mq, mk, mv, sq, sk, sv, cq, ck, cv = _in_proj(xf, lw, tabs, seq)
        o_a = _mla_attention(mq, mk, mv, batch, seq)
        o_b = _swa_attention(sq, sk, sv, slopes, lw["sinks"], batch, seq)
        o_c = _sb_attention(cq, ck, cv, tri, batch, seq)
        if layer % 2 == 0:
            xf, h = _out_proj(o_a, o_b, o_c, xf, lw, moe=False)
            xf = _dense_ffn(h, xf, lw)
        else:
            xf, h, sel = _out_proj(o_a, o_b, o_c, xf, lw, moe=True)
            xf = _moe_ffn(h, xf, sel, lw)
    return xf.reshape(batch, seq, D_MODEL)
```

```python
import functools

import jax
import jax.numpy as jnp
from jax import lax
from jax.experimental import pallas as pl
from jax.experimental.pallas import tpu as pltpu

F32 = jnp.float32
BF16 = jnp.bfloat16

D_MODEL = 1024
HEAD_DIM = 64
MLA_HEADS = 4
MLA_Q_RANK = 256
MLA_KV_RANK = 128
MLA_NOPE = 64
MLA_ROPE = 32
MLA_QK = MLA_NOPE + MLA_ROPE
MLA_V = 64
ROPE_THETA = 10000.0
SWA_Q_HEADS = 8
SWA_KV_HEADS = 2
SWA_WINDOW = 128
SB_HEADS = 4
WIDTH_A = MLA_HEADS * MLA_V
WIDTH_B = SWA_Q_HEADS * HEAD_DIM
WIDTH_C = SB_HEADS * HEAD_DIM
COLS_A = MLA_Q_RANK + MLA_KV_RANK + MLA_ROPE
COLS_B = (SWA_Q_HEADS + 2 * SWA_KV_HEADS) * HEAD_DIM
D_FF = 2816
N_EXPERTS = 8
D_FF_EXPERT = 3584
RMS_EPS = 1e-6
NEG = -1e30
LOG2E = 1.4426950408889634

LANES = 128
VMEM_LIMIT = 56 * 1024 * 1024

TM_PROJ = 512
TQ_ATTN = 256
TK_ATTN = 1024
TC_SB = 256
TQ_SWA = 512
TM_FFN = 512
TF_FFN = 1408
TM_MOE = 512
TF_MOE = 1792

P_CQ = 0
P_CKV = 256
P_KPE = 384
P_SWQ = 512
P_SWK = 1024
P_SWV = 1280
P_SBQ = 1536
P_SBK = 1792
P_SBV = 2048
P_COLS = 2304


def _dot(a, b):
    return jnp.dot(a, b, preferred_element_type=F32)


def _dot_nt(a, b):
    return lax.dot_general(a, b, (((1,), (1,)), ((), ())), preferred_element_type=F32)


def _rms(x, g):
    return x * lax.rsqrt(jnp.mean(x * x, axis=-1, keepdims=True) + RMS_EPS) * g


def _in_proj_kernel(x_ref, g_ref, w_ref, gcq_ref, wuq_ref, gckv_ref, wukv_ref, gq_ref, gk_ref,
                    cos_ref, sina_ref, sinb_ref, sgq_ref, sgk_ref,
                    mq_ref, mk_ref, mv_ref, sq_ref, sk_ref, sv_ref, cq_ref, ck_ref, cv_ref):
    h = _rms(x_ref[...], g_ref[...]).astype(BF16)

    def proj(lo, hi):
        return _dot(h, w_ref[:, lo:hi])

    cos = cos_ref[...]
    sina = sina_ref[...]
    sinb = sinb_ref[...]

    def rope(t):
        return t * cos + pltpu.roll(t, LANES - MLA_ROPE // 2, 1) * sina + pltpu.roll(t, MLA_ROPE // 2, 1) * sinb

    def head_norm(t, g):
        return t * lax.rsqrt(jnp.sum(t * t, axis=-1, keepdims=True) * (1.0 / MLA_QK) + RMS_EPS) * g

    a = proj(P_CQ, P_SWQ)
    q = _dot(_rms(a[:, :MLA_Q_RANK], gcq_ref[...]).astype(BF16), wuq_ref[...])
    for hh in range(MLA_HEADS):
        sl = slice(LANES * hh, LANES * (hh + 1))
        mq_ref[:, sl] = (rope(head_norm(q[:, sl], gq_ref[...])) * (MLA_QK ** -0.5 * LOG2E)).astype(BF16)
    kv = _dot(_rms(a[:, P_CKV:P_KPE], gckv_ref[...]).astype(BF16), wukv_ref[...])
    kpe = a[:, P_KPE:P_SWQ]
    for hh in range(MLA_HEADS):
        sl = slice(LANES * hh, LANES * (hh + 1))
        mk_ref[:, sl] = rope(head_norm(kv[:, sl] + kpe, gk_ref[...])).astype(BF16)
    mv_ref[...] = kv[:, MLA_HEADS * LANES:].astype(BF16)

    lane = lax.broadcasted_iota(jnp.int32, (1, LANES), 1)
    low = lane < HEAD_DIM

    def pair_norm(t, g):
        t2 = t * t
        lo = jnp.sum(jnp.where(low, t2, 0.0), axis=-1, keepdims=True)
        hi = jnp.sum(jnp.where(low, 0.0, t2), axis=-1, keepdims=True)
        ms = jnp.where(low, lo, hi) * (1.0 / HEAD_DIM)
        return t * lax.rsqrt(ms + RMS_EPS) * g

    scale = HEAD_DIM ** -0.5
    bq = proj(P_SWQ, P_SWK)
    for gi in range(WIDTH_B // LANES):
        sl = slice(LANES * gi, LANES * (gi + 1))
        sq_ref[:, sl] = (pair_norm(bq[:, sl], sgq_ref[...]) * scale).astype(BF16)
    bk = proj(P_SWK, P_SWV)
    for gi in range(SWA_KV_HEADS):
        sl = slice(LANES * gi, LANES * (gi + 1))
        sk_ref[:, sl] = pair_norm(bk[:, sl], sgk_ref[...]).astype(BF16)
    sv_ref[...] = proj(P_SWV, P_SBQ).astype(BF16)
    cq_ref[...] = (proj(P_SBQ, P_SBK) * (scale * LOG2E)).astype(BF16)
    ck_ref[...] = proj(P_SBK, P_SBV).astype(BF16)
    cv_ref[...] = proj(P_SBV, P_COLS).astype(BF16)


def _in_proj(x, lw, tabs, seq):
    n = x.shape[0]
    tm = TM_PROJ
    nt = seq // tm
    row = lambda i: (i, 0)
    fixed = lambda i: (0, 0)
    pos = lambda i: (i % nt, 0)
    full = lambda a: pl.BlockSpec(a.shape, fixed)
    cos, sina, sinb = tabs
    widths = (4 * LANES, 4 * LANES, WIDTH_A, WIDTH_B, 2 * LANES, 2 * LANES, WIDTH_C, WIDTH_C, WIDTH_C)
    return pl.pallas_call(
        _in_proj_kernel,
        grid=(n // tm,),
        in_specs=[pl.BlockSpec((tm, D_MODEL), row), full(lw["attn_g"]), full(lw["w_in"]), full(lw["cq_g"]),
                  full(lw["w_uq"]), full(lw["ckv_g"]), full(lw["w_ukv"]), full(lw["mla_qg"]), full(lw["mla_kg"]),
                  pl.BlockSpec((tm, LANES), pos), pl.BlockSpec((tm, LANES), pos), pl.BlockSpec((tm, LANES), pos),
                  full(lw["swa_qg"]), full(lw["swa_kg"])],
        out_specs=[pl.BlockSpec((tm, w), row) for w in widths],
        out_shape=[jax.ShapeDtypeStruct((n, w), BF16) for w in widths],
        compiler_params=pltpu.CompilerParams(dimension_semantics=("parallel",), vmem_limit_bytes=VMEM_LIMIT),
        name="in_proj",
    )(x, lw["attn_g"], lw["w_in"], lw["cq_g"], lw["w_uq"], lw["ckv_g"], lw["w_ukv"], lw["mla_qg"], lw["mla_kg"],
      cos, sina, sinb, lw["swa_qg"], lw["swa_kg"])


def _mla_kernel(q_ref, k_ref, v_ref, o_ref, sa_ref, sb_ref, m_ref, l_ref, acc_ref, *, tq, tk):
    i = pl.program_id(2)
    n_full = (i * tq) // tk
    m_ref[...] = jnp.full(m_ref.shape, NEG, F32)
    l_ref[...] = jnp.zeros(l_ref.shape, F32)
    acc_ref[...] = jnp.zeros(acc_ref.shape, F32)

    def scores(j, s_ref):
        start = pl.multiple_of(j * tk, tk)
        k = k_ref[pl.ds(start, tk), :]
        for hh in range(2):
            sl = slice(LANES * hh, LANES * (hh + 1))
            s_ref[hh] = _dot_nt(q_ref[:, sl], k[:, sl])

    def absorb(j, s_ref, masked):
        start = pl.multiple_of(j * tk, tk)
        v = v_ref[pl.ds(start, tk), :]
        if masked:
            diff = lax.broadcasted_iota(jnp.int32, (tq, tk), 1) - lax.broadcasted_iota(jnp.int32, (tq, tk), 0)
            keep = diff <= i * tq - start
        for hh in range(2):
            s = s_ref[hh]
            if masked:
                s = jnp.where(keep, s, NEG)
            m = m_ref[hh]
            m_new = jnp.maximum(m, jnp.max(s, axis=-1, keepdims=True))
            alpha = jnp.exp2(m - m_new)
            p = jnp.exp2(s - m_new)
            l_ref[hh] = alpha * l_ref[hh] + jnp.sum(p, axis=-1, keepdims=True)
            acc_ref[hh] = alpha * acc_ref[hh] + _dot(p.astype(BF16), v)
            m_ref[hh] = m_new

    scores(0, sa_ref)

    def pair(t, carry):
        scores(2 * t + 1, sb_ref)
        absorb(2 * t, sa_ref, False)
        scores(2 * t + 2, sa_ref)
        absorb(2 * t + 1, sb_ref, False)
        return carry

    lax.fori_loop(0, n_full // 2, pair, 0)
    base = 2 * (n_full // 2)

    @pl.when(n_full % 2 == 1)
    def _():
        scores(base + 1, sb_ref)
        absorb(base, sa_ref, False)
        absorb(base + 1, sb_ref, True)

    @pl.when(n_full % 2 == 0)
    def _():
        absorb(base, sa_ref, True)

    lane = lax.broadcasted_iota(jnp.int32, (1, LANES), 1)
    o_ref[...] = jnp.where(lane < MLA_V, acc_ref[0] / l_ref[0], acc_ref[1] / l_ref[1])


def _mla_attention(q, k, v, batch, seq):
    tq, tk = TQ_ATTN, TK_ATTN
    q = q.reshape(batch, seq, -1)
    k = k.reshape(batch, seq, -1)
    v = v.reshape(batch, seq, -1)
    out = pl.pallas_call(
        functools.partial(_mla_kernel, tq=tq, tk=tk),
        grid=(batch, MLA_HEADS // 2, seq // tq),
        in_specs=[pl.BlockSpec((None, tq, 2 * LANES), lambda b, p, i: (b, i, p)),
                  pl.BlockSpec((None, seq, 2 * LANES), lambda b, p, i: (b, 0, p)),
                  pl.BlockSpec((None, seq, LANES), lambda b, p, i: (b, 0, p))],
        out_specs=pl.BlockSpec((None, tq, LANES), lambda b, p, i: (b, i, p)),
        out_shape=jax.ShapeDtypeStruct((batch, seq, WIDTH_A), F32),
        scratch_shapes=[pltpu.VMEM((2, tq, tk), F32), pltpu.VMEM((2, tq, tk), F32),
                        pltpu.VMEM((2, tq, 1), F32), pltpu.VMEM((2, tq, 1), F32), pltpu.VMEM((2, tq, LANES), F32)],
        compiler_params=pltpu.CompilerParams(dimension_semantics=("parallel", "parallel", "arbitrary"),
                                             vmem_limit_bytes=VMEM_LIMIT),
        name="mla_attention",
    )(q, k, v)
    return out.reshape(batch * seq, WIDTH_A)


def _sb_kernel(q_ref, k_ref, v_ref, tri_ref, o_ref, za_ref, zb_ref, c_ref, acc_ref, *, tq, tk, tc):
    i = pl.program_id(2)
    n_full = (i * tq) // tk
    lane = lax.broadcasted_iota(jnp.int32, (1, LANES), 1)
    low = lane < HEAD_DIM
    c_ref[...] = jnp.zeros(c_ref.shape, F32)
    acc_ref[...] = jnp.zeros(acc_ref.shape, F32)
    sign = jnp.uint32(0x80000000)

    def logits(j, z_ref):
        start = pl.multiple_of(j * tk, tk)
        k = k_ref[pl.ds(start, tk), :]
        q = q_ref[...]
        zero = jnp.zeros_like(q)
        z_ref[0] = _dot_nt(jnp.where(low, q, zero), k)
        z_ref[1] = _dot_nt(jnp.where(low, zero, q), k)

    def absorb(j, z_ref, masked):
        start = pl.multiple_of(j * tk, tk)
        v = v_ref[pl.ds(start, tk), :]
        tri2 = tri_ref[...]
        if masked:
            diff = lax.broadcasted_iota(jnp.int32, (tq, tk), 1) - lax.broadcasted_iota(jnp.int32, (tq, tk), 0)
            keep = diff < i * tq - start
        staged = []
        for hh in range(2):
            z = z_ref[hh]
            neg_abs = lax.bitcast_convert_type(lax.bitcast_convert_type(z, jnp.uint32) | sign, F32)
            sp = jnp.maximum(z, 0.0) + jnp.log2(1.0 + jnp.exp2(neg_abs))
            if masked:
                sp = jnp.where(keep, sp, 0.0)
            c = c_ref[hh]
            parts = []
            for cc in reversed(range(tk // tc)):
                cs = slice(cc * tc, (cc + 1) * tc)
                spc = sp[:, cs]
                hi = spc.astype(BF16)
                lo = (spc - hi.astype(F32)).astype(BF16)
                right = _dot(jnp.concatenate([hi, lo], axis=1), tri2)
                parts.append((cs, z[:, cs] - spc, right, c))
                c = c + right[:, 0:1] + spc[:, 0:1]
            c_ref[hh] = c
            staged.append(parts)
        for hh in range(2):
            acc = acc_ref[hh]
            for cs, log_beta, right, c in staged[hh]:
                a = jnp.exp2(log_beta - right - c)
                if masked:
                    a = jnp.where(keep[:, cs], a, 0.0)
                acc = acc + _dot(a.astype(BF16), v[cs, :])
            acc_ref[hh] = acc

    def pairs(first_ref, second_ref):
        npair = n_full // 2

        def body(t, carry):
            top = 2 * (npair - t) - 1
            logits(top - 1, second_ref)
            absorb(top, first_ref, False)
            logits(jnp.maximum(top - 2, 0), first_ref)
            absorb(top - 1, second_ref, False)
            return carry

        lax.fori_loop(0, npair, body, 0)

    logits(n_full, za_ref)

    @pl.when(n_full % 2 == 1)
    def _():
        logits(n_full - 1, zb_ref)
        absorb(n_full, za_ref, True)
        logits(jnp.maximum(n_full - 2, 0), za_ref)
        absorb(n_full - 1, zb_ref, False)
        pairs(za_ref, zb_ref)

    @pl.when(n_full % 2 == 0)
    def _():
        logits(jnp.maximum(n_full - 1, 0), zb_ref)
        absorb(n_full, za_ref, True)
        pairs(zb_ref, za_ref)

    o_ref[...] = jnp.where(low, acc_ref[0], acc_ref[1])


def _sb_attention(q, k, v, tri2, batch, seq):
    tq, tk, tc = TQ_ATTN, TK_ATTN, TC_SB
    q = q.reshape(batch, seq, -1)
    k = k.reshape(batch, seq, -1)
    v = v.reshape(batch, seq, -1)
    out = pl.pallas_call(
        functools.partial(_sb_kernel, tq=tq, tk=tk, tc=tc),
        grid=(batch, SB_HEADS // 2, seq // tq),
        in_specs=[pl.BlockSpec((None, tq, LANES), lambda b, p, i: (b, i, p)),
                  pl.BlockSpec((None, seq, LANES), lambda b, p, i: (b, 0, p)),
                  pl.BlockSpec((None, seq, LANES), lambda b, p, i: (b, 0, p)),
                  pl.BlockSpec((2 * tc, tc), lambda b, p, i: (0, 0))],
        out_specs=pl.BlockSpec((None, tq, LANES), lambda b, p, i: (b, i, p)),
        out_shape=jax.ShapeDtypeStruct((batch, seq, WIDTH_C), F32),
        scratch_shapes=[pltpu.VMEM((2, tq, tk), F32), pltpu.VMEM((2, tq, tk), F32),
                        pltpu.VMEM((2, tq, 1), F32), pltpu.VMEM((2, tq, LANES), F32)],
        compiler_params=pltpu.CompilerParams(dimension_semantics=("parallel", "parallel", "arbitrary"),
                                             vmem_limit_bytes=VMEM_LIMIT),
        name="sb_attention",
    )(q, k, v, tri2)
    return out.reshape(batch * seq, WIDTH_C)


def _swa_kernel(slopes_ref, sinks_ref, q_ref, k_ref, v_ref, o_ref, *, tq):
    w = SWA_WINDOW
    group = SWA_Q_HEADS // SWA_KV_HEADS
    hkv = pl.program_id(1)
    i = pl.program_id(2)
    lane = lax.broadcasted_iota(jnp.int32, (1, LANES), 1)
    low = lane < HEAD_DIM
    row = lax.broadcasted_iota(jnp.int32, (w, 2 * w), 0)
    col = lax.broadcasted_iota(jnp.int32, (w, 2 * w), 1)
    for n in range(tq // w):
        q0 = i * tq + n * w
        ks = pl.multiple_of(jnp.maximum(q0 - w, 0), w)
        k = k_ref[pl.ds(ks, 2 * w), :]
        v = v_ref[pl.ds(ks, 2 * w), :]
        dist = (q0 - ks) + row - col
        valid = (dist >= 0) & (dist < w)
        distf = dist.astype(F32)
        for pr in range(group // 2):
            qp = q_ref[n * w:(n + 1) * w, LANES * pr:LANES * (pr + 1)]
            zero = jnp.zeros_like(qp)
            outs = []
            for half in range(2):
                head = hkv * group + pr * 2 + half
                qh = jnp.where(low, qp, zero) if half == 0 else jnp.where(low, zero, qp)
                s = _dot_nt(qh, k) - slopes_ref[head] * distf
                s = jnp.where(valid, s, NEG)
                sink = sinks_ref[head]
                m = jnp.maximum(jnp.max(s, axis=-1, keepdims=True), sink)
                p = jnp.exp(s - m)
                den = jnp.sum(p, axis=-1, keepdims=True) + jnp.exp(sink - m)
                outs.append(_dot(p.astype(BF16), v) / den)
            o_ref[n * w:(n + 1) * w, LANES * pr:LANES * (pr + 1)] = jnp.where(low, outs[0], outs[1])


def _swa_attention(q, k, v, slopes, sinks, batch, seq):
    tq = TQ_SWA
    q = q.reshape(batch, seq, -1)
    k = k.reshape(batch, seq, -1)
    v = v.reshape(batch, seq, -1)
    smem = pl.BlockSpec(memory_space=pltpu.SMEM)
    out = pl.pallas_call(
        functools.partial(_swa_kernel, tq=tq),
        grid=(batch, SWA_KV_HEADS, seq // tq),
        in_specs=[smem, smem,
                  pl.BlockSpec((None, tq, 2 * LANES), lambda b, h, i: (b, i, h)),
                  pl.BlockSpec((None, seq, LANES), lambda b, h, i: (b, 0, h)),
                  pl.BlockSpec((None, seq, LANES), lambda b, h, i: (b, 0, h))],
        out_specs=pl.BlockSpec((None, tq, 2 * LANES), lambda b, h, i: (b, i, h)),
        out_shape=jax.ShapeDtypeStruct((batch, seq, WIDTH_B), F32),
        compiler_params=pltpu.CompilerParams(dimension_semantics=("parallel", "parallel", "arbitrary"),
                                             vmem_limit_bytes=VMEM_LIMIT),
        name="swa_attention",
    )(slopes, sinks, q, k, v)
    return out.reshape(batch * seq, WIDTH_B)


def _out_proj_kernel(*refs, moe):
    if moe:
        oa, ob, oc, x_ref, gout, wout, gffn, wr_hi, wr_lo, xo_ref, h_ref, sel_ref = refs
    else:
        oa, ob, oc, x_ref, gout, wout, gffn, xo_ref, h_ref = refs
    g = gout[...]
    b0, b1 = WIDTH_A, WIDTH_A + WIDTH_B
    na = _rms(oa[...], g[:, :b0]).astype(BF16)
    nb = _rms(ob[...], g[:, b0:b1]).astype(BF16)
    nc = _rms(oc[...], g[:, b1:]).astype(BF16)
    xn = x_ref[...] + _dot(na, wout[:b0, :]) + _dot(nb, wout[b0:b1, :]) + _dot(nc, wout[b1:, :])
    xo_ref[...] = xn
    hf = _rms(xn, gffn[...])
    hb = hf.astype(BF16)
    h_ref[...] = hb
    if moe:
        lo = (hf - hb.astype(F32)).astype(BF16)
        logits = _dot(hb, wr_hi[...]) + _dot(lo, wr_hi[...]) + _dot(hb, wr_lo[...])
        lane = lax.broadcasted_iota(jnp.int32, logits.shape, 1)
        lg = jnp.where(lane < N_EXPERTS, logits, -jnp.inf)
        m1 = jnp.max(lg, axis=-1, keepdims=True)
        i1 = jnp.min(jnp.where(lg == m1, lane, LANES), axis=-1, keepdims=True)
        lg2 = jnp.where(lane == i1, -jnp.inf, lg)
        m2 = jnp.max(lg2, axis=-1, keepdims=True)
        i2 = jnp.min(jnp.where(lg2 == m2, lane, LANES), axis=-1, keepdims=True)
        e = jnp.exp(m2 - m1)
        g1 = 1.0 / (1.0 + e)
        g2 = e / (1.0 + e)
        sel_ref[...] = jnp.where(lane == 0, i1.astype(F32),
                                 jnp.where(lane == 1, i2.astype(F32),
                                           jnp.where(lane == 2, g1, jnp.where(lane == 3, g2, 0.0))))


def _out_proj(oa, ob, oc, x, lw, moe):
    n = x.shape[0]
    tm = TM_PROJ
    row = lambda i: (i, 0)
    fixed = lambda i: (0, 0)
    full = lambda a: pl.BlockSpec(a.shape, fixed)
    args = [oa, ob, oc, x, lw["out_g"], lw["w_out"], lw["ffn_g"]]
    in_specs = [pl.BlockSpec((tm, WIDTH_A), row), pl.BlockSpec((tm, WIDTH_B), row), pl.BlockSpec((tm, WIDTH_C), row),
                pl.BlockSpec((tm, D_MODEL), row), full(lw["out_g"]), full(lw["w_out"]), full(lw["ffn_g"])]
    out_specs = [pl.BlockSpec((tm, D_MODEL), row), pl.BlockSpec((tm, D_MODEL), row)]
    out_shape = [jax.ShapeDtypeStruct((n, D_MODEL), F32), jax.ShapeDtypeStruct((n, D_MODEL), BF16)]
    if moe:
        args += [lw["router_hi"], lw["router_lo"]]
        in_specs += [full(lw["router_hi"]), full(lw["router_lo"])]
        out_specs.append(pl.BlockSpec((tm, LANES), row))
        out_shape.append(jax.ShapeDtypeStruct((n, LANES), F32))
    return pl.pallas_call(
        functools.partial(_out_proj_kernel, moe=moe),
        grid=(n // tm,),
        in_specs=in_specs, out_specs=out_specs, out_shape=out_shape,
        compiler_params=pltpu.CompilerParams(dimension_semantics=("parallel",), vmem_limit_bytes=VMEM_LIMIT),
        name="out_proj_moe" if moe else "out_proj",
    )(*args)


def _swiglu_partial(h, wg, wu, wd):
    g = _dot(h, wg)
    u = _dot(h, wu)
    return _dot((g * jax.nn.sigmoid(g) * u).astype(BF16), wd)


def _ffn_kernel(h_ref, wg_ref, wu_ref, wd_ref, x_ref, o_ref):
    j = pl.program_id(1)
    y = _swiglu_partial(h_ref[...], wg_ref[...], wu_ref[...], wd_ref[...])

    @pl.when(j == 0)
    def _():
        o_ref[...] = x_ref[...] + y

    @pl.when(j > 0)
    def _():
        o_ref[...] += y


def _dense_ffn(h, x, lw):
    n = x.shape[0]
    tm, tf = TM_FFN, TF_FFN
    return pl.pallas_call(
        _ffn_kernel,
        grid=(n // tm, D_FF // tf),
        in_specs=[pl.BlockSpec((tm, D_MODEL), lambda i, j: (i, 0)),
                  pl.BlockSpec((D_MODEL, tf), lambda i, j: (0, j)),
                  pl.BlockSpec((D_MODEL, tf), lambda i, j: (0, j)),
                  pl.BlockSpec((tf, D_MODEL), lambda i, j: (j, 0)),
                  pl.BlockSpec((tm, D_MODEL), lambda i, j: (i, 0))],
        out_specs=pl.BlockSpec((tm, D_MODEL), lambda i, j: (i, 0)),
        out_shape=jax.ShapeDtypeStruct((n, D_MODEL), F32),
        compiler_params=pltpu.CompilerParams(dimension_semantics=("parallel", "arbitrary"),
                                             vmem_limit_bytes=VMEM_LIMIT),
        name="dense_ffn",
    )(h, lw["w_gate"], lw["w_up"], lw["w_down"], x)


def _moe_kernel(blk_e_ref, n_used_ref, xs_ref, wg_ref, wu_ref, wd_ref, o_ref):
    i = pl.program_id(0)
    j = pl.program_id(1)
    used = i < n_used_ref[0]

    @pl.when(used)
    def _():
        y = _swiglu_partial(xs_ref[...], wg_ref[...], wu_ref[...], wd_ref[...])

        @pl.when(j == 0)
        def _():
            o_ref[...] = y

        @pl.when(j > 0)
        def _():
            o_ref[...] += y

    @pl.when(jnp.logical_not(used) & (j == 0))
    def _():
        o_ref[...] = jnp.zeros_like(o_ref)


def _moe_experts(xs, blk_e, n_used, lw):
    p = xs.shape[0]
    tm, tf = TM_MOE, TF_MOE
    n_ff = D_FF_EXPERT // tf

    def col(i, j):
        return jnp.where(i % 2 == 0, j, n_ff - 1 - j)

    grid_spec = pltpu.PrefetchScalarGridSpec(
        num_scalar_prefetch=2,
        grid=(p // tm, n_ff),
        in_specs=[pl.BlockSpec((tm, D_MODEL), lambda i, j, be, nu: (i, 0)),
                  pl.BlockSpec((None, D_MODEL, tf), lambda i, j, be, nu: (be[i], 0, col(i, j))),
                  pl.BlockSpec((None, D_MODEL, tf), lambda i, j, be, nu: (be[i], 0, col(i, j))),
                  pl.BlockSpec((None, tf, D_MODEL), lambda i, j, be, nu: (be[i], col(i, j), 0))],
        out_specs=pl.BlockSpec((tm, D_MODEL), lambda i, j, be, nu: (i, 0)),
    )
    return pl.pallas_call(
        _moe_kernel,
        grid_spec=grid_spec,
        out_shape=jax.ShapeDtypeStruct((p, D_MODEL), F32),
        compiler_params=pltpu.CompilerParams(dimension_semantics=("arbitrary", "arbitrary"),
                                             vmem_limit_bytes=VMEM_LIMIT),
        name="moe_experts",
    )(blk_e, n_used, xs, lw["w_gate"], lw["w_up"], lw["w_down"])


def _moe_ffn(h, x, sel, lw):
    n = x.shape[0]
    tm = TM_MOE
    top_idx = sel[:, :2].astype(jnp.int32)
    gates = sel[:, 2:4]
    flat_e = top_idx.reshape(-1)
    onehot = (flat_e[:, None] == jnp.arange(N_EXPERTS, dtype=jnp.int32)[None, :]).astype(jnp.int32)
    csum = jnp.cumsum(onehot, axis=0)
    rank = jnp.sum(csum * onehot, axis=1) - 1
    counts = csum[-1]
    padded = (counts + tm - 1) // tm * tm
    pend = jnp.cumsum(padded)
    pstart = pend - padded
    dest = pstart[flat_e] + rank
    p = n * 2 + N_EXPERTS * tm
    n_blk = p // tm
    slot_tok = jnp.zeros((p,), jnp.int32).at[dest].set(jnp.arange(2 * n, dtype=jnp.int32) // 2)
    blk_start = jnp.arange(n_blk, dtype=jnp.int32) * tm
    blk_e = jnp.minimum(jnp.sum((pend[None, :] <= blk_start[:, None]).astype(jnp.int32), axis=1), N_EXPERTS - 1)
    n_used = (pend[-1:] // tm).astype(jnp.int32)
    xs = h[slot_tok]
    ys = _moe_experts(xs, blk_e, n_used, lw)
    dest2 = dest.reshape(n, 2)
    return x + ys[dest2[:, 0]] * gates[:, 0:1] + ys[dest2[:, 1]] * gates[:, 1:2]


def _pad_cols(w, width):
    return jnp.pad(w, ((0, 0), (0, width - w.shape[1])))


def _head_groups(w, heads, dim):
    kdim = w.shape[0]
    return jnp.pad(w.reshape(kdim, heads, dim), ((0, 0), (0, 0), (0, LANES - dim))).reshape(kdim, heads * LANES)


def _layer_weights(layer, p):
    w_in = p["w_in"][layer]
    a, b, c = w_in[:, :COLS_A], w_in[:, COLS_A:COLS_A + COLS_B], w_in[:, COLS_A + COLS_B:]
    kpe = jnp.pad(a[:, MLA_Q_RANK + MLA_KV_RANK:], ((0, 0), (MLA_NOPE, LANES - MLA_QK)))
    bq = b[:, :WIDTH_B]

    def dup(w):
        return jnp.repeat(w.reshape(D_MODEL, SWA_KV_HEADS, 1, HEAD_DIM), 2, axis=2).reshape(D_MODEL, 2 * LANES)

    bk = dup(b[:, WIDTH_B:WIDTH_B + SWA_KV_HEADS * HEAD_DIM])
    bv = dup(b[:, WIDTH_B + SWA_KV_HEADS * HEAD_DIM:])
    w_in_p = jnp.concatenate([a[:, :MLA_Q_RANK + MLA_KV_RANK], kpe, bq, bk, bv, c], axis=1).astype(BF16)
    w_ukv = p["mla_w_ukv"][layer].reshape(MLA_KV_RANK, MLA_HEADS, MLA_NOPE + MLA_V)
    w_uk = _head_groups(w_ukv[:, :, :MLA_NOPE].reshape(MLA_KV_RANK, -1), MLA_HEADS, MLA_NOPE)
    w_uv = w_ukv[:, :, MLA_NOPE:].reshape(MLA_KV_RANK, -1)
    row = lambda g: g.reshape(1, -1).astype(F32)
    lw = {
        "attn_g": row(p["attn_norm_g"][layer]),
        "w_in": w_in_p,
        "cq_g": row(p["mla_cq_norm_g"][layer]),
        "w_uq": _head_groups(p["mla_w_uq"][layer], MLA_HEADS, MLA_QK).astype(BF16),
        "ckv_g": row(p["mla_ckv_norm_g"][layer]),
        "w_ukv": jnp.concatenate([w_uk, w_uv], axis=1).astype(BF16),
        "mla_qg": _pad_cols(row(p["mla_qn_g"][layer]), LANES),
        "mla_kg": _pad_cols(row(p["mla_kn_g"][layer]), LANES),
        "swa_qg": jnp.tile(row(p["swa_qn_g"][layer]), (1, 2)),
        "swa_kg": jnp.tile(row(p["swa_kn_g"][layer]), (1, 2)),
        "sinks": p["swa_sinks"][layer].astype(F32),
        "out_g": row(p["out_norm_g"][layer]),
        "w_out": p["w_out"][layer].astype(BF16),
        "ffn_g": row(p["ffn_norm_g"][layer]),
    }
    j = layer // 2
    if layer % 2 == 0:
        lw["w_gate"] = p["dense_w_gate"][j].astype(BF16)
        lw["w_up"] = p["dense_w_up"][j].astype(BF16)
        lw["w_down"] = p["dense_w_down"][j].astype(BF16)
    else:
        rw = _pad_cols(p["router_w"][j], LANES)
        hi = rw.astype(BF16)
        lw["router_hi"] = hi
        lw["router_lo"] = (rw - hi.astype(F32)).astype(BF16)
        lw["w_gate"] = p["moe_w_gate"][j].astype(BF16)
        lw["w_up"] = p["moe_w_up"][j].astype(BF16)
        lw["w_down"] = p["moe_w_down"][j].astype(BF16)
    return lw


def _rope_tables(seq):
    half = MLA_ROPE // 2
    inv = ROPE_THETA ** (-jnp.arange(half, dtype=F32) / half)
    ang = jnp.arange(seq).astype(F32)[:, None] * inv[None, :]
    cos, sin = jnp.cos(ang), jnp.sin(ang)
    zl = jnp.zeros((seq, MLA_NOPE), F32)
    zh = jnp.zeros((seq, half), F32)
    zr = jnp.zeros((seq, LANES - MLA_QK), F32)
    cos_t = jnp.concatenate([zl + 1.0, cos, cos, zr + 1.0], axis=1)
    sina = jnp.concatenate([zl, -sin, zh, zr], axis=1)
    sinb = jnp.concatenate([zl, zh, sin, zr], axis=1)
    return cos_t, sina, sinb


def kernel(x, attn_norm_g, w_in, mla_cq_norm_g, mla_w_uq, mla_ckv_norm_g, mla_w_ukv, mla_qn_g, mla_kn_g, swa_qn_g, swa_kn_g, swa_sinks, out_norm_g, w_out, ffn_norm_g, dense_w_gate, dense_w_up, dense_w_down, router_w, moe_w_gate, moe_w_up, moe_w_down):
    params = dict(attn_norm_g=attn_norm_g, w_in=w_in, mla_cq_norm_g=mla_cq_norm_g, mla_w_uq=mla_w_uq,
                  mla_ckv_norm_g=mla_ckv_norm_g, mla_w_ukv=mla_w_ukv, mla_qn_g=mla_qn_g, mla_kn_g=mla_kn_g,
                  swa_qn_g=swa_qn_g, swa_kn_g=swa_kn_g, swa_sinks=swa_sinks, out_norm_g=out_norm_g, w_out=w_out,
                  ffn_norm_g=ffn_norm_g, dense_w_gate=dense_w_gate, dense_w_up=dense_w_up,
                  dense_w_down=dense_w_down, router_w=router_w, moe_w_gate=moe_w_gate, moe_w_up=moe_w_up,
                  moe_w_down=moe_w_down)
    batch, seq, _ = x.shape
    depth = w_in.shape[0]
    tabs = _rope_tables(seq)
    slopes = jnp.exp2(-8.0 * jnp.arange(1, SWA_Q_HEADS + 1, dtype=F32) / SWA_Q_HEADS)
    tri = (jnp.arange(TC_SB)[:, None] > jnp.arange(TC_SB)[None, :]).astype(BF16)
    tri = jnp.concatenate([tri, tri], axis=0)
    xf =x.reshape(batch * seq, D_MODEL)
    for layer in range(depth):
        lw = _layer_weights(layer, params)
        mq, mk, mv, sq, sk, sv, cq, ck, cv = _in_proj(xf, lw, tabs, seq)
        o_a = _mla_attention(mq, mk, mv, batch, seq)
        o_b = _swa_attention(sq, sk, sv, slopes, lw["sinks"], batch, seq)
        o_c = _sb_attention(cq, ck, cv, tri, batch, seq)
        if layer % 2 == 0:
            xf, h = _out_proj(o_a, o_b, o_c, xf, lw, moe=False)
            xf = _dense_ffn(h, xf, lw)
        else:
            xf, h, sel = _out_proj(o_a, o_b, o_c, xf, lw, moe=True)
            xf = _moe_ffn(h, xf, sel, lw)
    return xf.reshape(batch, seq, D_MODEL)
```

```python
import functools

import jax
import jax.numpy as jnp
from jax import lax
from jax.experimental import pallas as pl
from jax.experimental.pallas import tpu as pltpu

F32 = jnp.float32
BF16 = jnp.bfloat16

D_MODEL = 1024
HEAD_DIM = 64
MLA_HEADS = 4
MLA_Q_RANK = 256
MLA_KV_RANK = 128
MLA_NOPE = 64
MLA_ROPE = 32
MLA_QK = MLA_NOPE + MLA_ROPE
MLA_V = 64
ROPE_THETA = 10000.0
SWA_Q_HEADS = 8
SWA_KV_HEADS = 2
SWA_WINDOW = 128
SB_HEADS = 4
WIDTH_A = MLA_HEADS * MLA_V
WIDTH_B = SWA_Q_HEADS * HEAD_DIM
WIDTH_C = SB_HEADS * HEAD_DIM
COLS_A = MLA_Q_RANK + MLA_KV_RANK + MLA_ROPE
COLS_B = (SWA_Q_HEADS + 2 * SWA_KV_HEADS) * HEAD_DIM
D_FF = 2816
N_EXPERTS = 8
D_FF_EXPERT = 3584
RMS_EPS = 1e-6
NEG = -1e30
LOG2E = 1.4426950408889634

LANES = 128
VMEM_LIMIT = 56 * 1024 * 1024

TM_PROJ = 512
TQ_ATTN = 256
TK_ATTN = 1024
TC_SB = 256
TQ_SWA = 512
TM_FFN = 512
TF_FFN = 1408
TM_MOE = 512
TM_ROUTE = 512
TF_MOE = 1792

P_CQ = 0
P_CKV = 256
P_KPE = 384
P_SWQ = 512
P_SWK = 1024
P_SWV = 1280
P_SBQ = 1536
P_SBK = 1792
P_SBV = 2048
P_COLS = 2304


def _dot(a, b):
    return jnp.dot(a, b, preferred_element_type=F32)


def _dot_nt(a, b):
    return lax.dot_general(a, b, (((1,), (1,)), ((), ())), preferred_element_type=F32)


def _rms(x, g):
    return x * lax.rsqrt(jnp.mean(x * x, axis=-1, keepdims=True) + RMS_EPS) * g


def _in_proj_kernel(x_ref, g_ref, w_ref, gcq_ref, wuq_ref, gckv_ref, wukv_ref, gq_ref, gk_ref,
                    cos_ref, sina_ref, sinb_ref, sgq_ref, sgk_ref,
                    mq_ref, mk_ref, mv_ref, sq_ref, sk_ref, sv_ref, cq_ref, ck_ref, cv_ref):
    h = _rms(x_ref[...], g_ref[...]).astype(BF16)

    def proj(lo, hi):
        return _dot(h, w_ref[:, lo:hi])

    cos = cos_ref[...]
    sina = sina_ref[...]
    sinb = sinb_ref[...]

    def rope(t):
        return t * cos + pltpu.roll(t, LANES - MLA_ROPE // 2, 1) * sina + pltpu.roll(t, MLA_ROPE // 2, 1) * sinb

    def head_norm(t, g):
        return t * lax.rsqrt(jnp.sum(t * t, axis=-1, keepdims=True) * (1.0 / MLA_QK) + RMS_EPS) * g

    a = proj(P_CQ, P_SWQ)
    q = _dot(_rms(a[:, :MLA_Q_RANK], gcq_ref[...]).astype(BF16), wuq_ref[...])
    for hh in range(MLA_HEADS):
        sl = slice(LANES * hh, LANES * (hh + 1))
        mq_ref[:, sl] = (rope(head_norm(q[:, sl], gq_ref[...])) * (MLA_QK ** -0.5 * LOG2E)).astype(BF16)
    kv = _dot(_rms(a[:, P_CKV:P_KPE], gckv_ref[...]).astype(BF16), wukv_ref[...])
    kpe = a[:, P_KPE:P_SWQ]
    for hh in range(MLA_HEADS):
        sl = slice(LANES * hh, LANES * (hh + 1))
        mk_ref[:, sl] = rope(head_norm(kv[:, sl] + kpe, gk_ref[...])).astype(BF16)
    mv_ref[...] = kv[:, MLA_HEADS * LANES:].astype(BF16)

    lane = lax.broadcasted_iota(jnp.int32, (1, LANES), 1)
    low = lane < HEAD_DIM

    def pair_norm(t, g):
        t2 = t * t
        lo = jnp.sum(jnp.where(low, t2, 0.0), axis=-1, keepdims=True)
        hi = jnp.sum(jnp.where(low, 0.0, t2), axis=-1, keepdims=True)
        ms = jnp.where(low, lo, hi) * (1.0 / HEAD_DIM)
        return t * lax.rsqrt(ms + RMS_EPS) * g

    scale = HEAD_DIM ** -0.5
    bq = proj(P_SWQ, P_SWK)
    for gi in range(WIDTH_B // LANES):
        sl = slice(LANES * gi, LANES * (gi + 1))
        sq_ref[:, sl] = (pair_norm(bq[:, sl], sgq_ref[...]) * scale).astype(BF16)
    bk = proj(P_SWK, P_SWV)
    for gi in range(SWA_KV_HEADS):
        sl = slice(LANES * gi, LANES * (gi + 1))
        sk_ref[:, sl] = pair_norm(bk[:, sl], sgk_ref[...]).astype(BF16)
    sv_ref[...] = proj(P_SWV, P_SBQ).astype(BF16)
    cq_ref[...] = (proj(P_SBQ, P_SBK) * (scale * LOG2E)).astype(BF16)
    ck_ref[...] = proj(P_SBK, P_SBV).astype(BF16)
    cv_ref[...] = proj(P_SBV, P_COLS).astype(BF16)


def _in_proj(x, lw, tabs, seq):
    n = x.shape[0]
    tm = TM_PROJ
    nt = seq // tm
    row = lambda i: (i, 0)
    fixed = lambda i: (0, 0)
    pos = lambda i: (i % nt, 0)
    full = lambda a: pl.BlockSpec(a.shape, fixed)
    cos, sina, sinb = tabs
    widths = (4 * LANES, 4 * LANES, WIDTH_A, WIDTH_B, 2 * LANES, 2 * LANES, WIDTH_C, WIDTH_C, WIDTH_C)
    return pl.pallas_call(
        _in_proj_kernel,
        grid=(n // tm,),
        in_specs=[pl.BlockSpec((tm, D_MODEL), row), full(lw["attn_g"]), full(lw["w_in"]), full(lw["cq_g"]),
                  full(lw["w_uq"]), full(lw["ckv_g"]), full(lw["w_ukv"]), full(lw["mla_qg"]), full(lw["mla_kg"]),
                  pl.BlockSpec((tm, LANES), pos), pl.BlockSpec((tm, LANES), pos), pl.BlockSpec((tm, LANES), pos),
                  full(lw["swa_qg"]), full(lw["swa_kg"])],
        out_specs=[pl.BlockSpec((tm, w), row) for w in widths],
        out_shape=[jax.ShapeDtypeStruct((n, w), BF16) for w in widths],
        compiler_params=pltpu.CompilerParams(dimension_semantics=("parallel",), vmem_limit_bytes=VMEM_LIMIT),
        name="in_proj",
    )(x, lw["attn_g"], lw["w_in"], lw["cq_g"], lw["w_uq"], lw["ckv_g"], lw["w_ukv"], lw["mla_qg"], lw["mla_kg"],
      cos, sina, sinb, lw["swa_qg"], lw["swa_kg"])


def _mla_kernel(q_ref, k_ref, v_ref, o_ref, sa_ref, sb_ref, m_ref, l_ref, acc_ref, *, tq, tk):
    i = pl.program_id(2)
    n_full = (i * tq) // tk
    m_ref[...] = jnp.full(m_ref.shape, NEG, F32)
    l_ref[...] = jnp.zeros(l_ref.shape, F32)
    acc_ref[...] = jnp.zeros(acc_ref.shape, F32)

    def scores(j, s_ref):
        start = pl.multiple_of(j * tk, tk)
        k = k_ref[pl.ds(start, tk), :]
        for hh in range(2):
            sl = slice(LANES * hh, LANES * (hh + 1))
            s_ref[hh] = _dot_nt(q_ref[:, sl], k[:, sl])

    def absorb(j, s_ref, masked):
        start = pl.multiple_of(j * tk, tk)
        v = v_ref[pl.ds(start, tk), :]
        if masked:
            diff = lax.broadcasted_iota(jnp.int32, (tq, tk), 1) - lax.broadcasted_iota(jnp.int32, (tq, tk), 0)
            keep = diff <= i * tq - start
        for hh in range(2):
            s = s_ref[hh]
            if masked:
                s = jnp.where(keep, s, NEG)
            m = m_ref[hh]
            m_new = jnp.maximum(m, jnp.max(s, axis=-1, keepdims=True))
            alpha = jnp.exp2(m - m_new)
            p = jnp.exp2(s - m_new)
            l_ref[hh] = alpha * l_ref[hh] + jnp.sum(p, axis=-1, keepdims=True)
            acc_ref[hh] = alpha * acc_ref[hh] + _dot(p.astype(BF16), v)
            m_ref[hh] = m_new

    scores(0, sa_ref)

    def pair(t, carry):
        scores(2 * t + 1, sb_ref)
        absorb(2 * t, sa_ref, False)
        scores(2 * t + 2, sa_ref)
        absorb(2 * t + 1, sb_ref, False)
        return carry

    lax.fori_loop(0, n_full // 2, pair, 0)
    base = 2 * (n_full // 2)

    @pl.when(n_full % 2 == 1)
    def _():
        scores(base + 1, sb_ref)
        absorb(base, sa_ref, False)
        absorb(base + 1, sb_ref, True)

    @pl.when(n_full % 2 == 0)
    def _():
        absorb(base, sa_ref, True)

    lane = lax.broadcasted_iota(jnp.int32, (1, LANES), 1)
    o_ref[...] = jnp.where(lane < MLA_V, acc_ref[0] / l_ref[0], acc_ref[1] / l_ref[1])


def _mla_attention(q, k, v, batch, seq):
    tq, tk = TQ_ATTN, TK_ATTN
    q = q.reshape(batch, seq, -1)
    k = k.reshape(batch, seq, -1)
    v = v.reshape(batch, seq, -1)
    out = pl.pallas_call(
        functools.partial(_mla_kernel, tq=tq, tk=tk),
        grid=(batch, MLA_HEADS // 2, seq // tq),
        in_specs=[pl.BlockSpec((None, tq, 2 * LANES), lambda b, p, i: (b, i, p)),
                  pl.BlockSpec((None, seq, 2 * LANES), lambda b, p, i: (b, 0, p)),
                  pl.BlockSpec((None, seq, LANES), lambda b, p, i: (b, 0, p))],
        out_specs=pl.BlockSpec((None, tq, LANES), lambda b, p, i: (b, i, p)),
        out_shape=jax.ShapeDtypeStruct((batch, seq, WIDTH_A), F32),
        scratch_shapes=[pltpu.VMEM((2, tq, tk), F32), pltpu.VMEM((2, tq, tk), F32),
                        pltpu.VMEM((2, tq, 1), F32), pltpu.VMEM((2, tq, 1), F32), pltpu.VMEM((2, tq, LANES), F32)],
        compiler_params=pltpu.CompilerParams(dimension_semantics=("parallel", "parallel", "arbitrary"),
                                             vmem_limit_bytes=VMEM_LIMIT),
        name="mla_attention",
    )(q, k, v)
    return out.reshape(batch * seq, WIDTH_A)


def _sb_kernel(q_ref, k_ref, v_ref, tri_ref, o_ref, za_ref, zb_ref, c_ref, acc_ref, *, tq, tk, tc):
    i = pl.program_id(2)
    n_full = (i * tq) // tk
    lane = lax.broadcasted_iota(jnp.int32, (1, LANES), 1)
    low = lane < HEAD_DIM
    c_ref[...] = jnp.zeros(c_ref.shape, F32)
    acc_ref[...] = jnp.zeros(acc_ref.shape, F32)
    sign = jnp.uint32(0x80000000)

    def logits(j, z_ref):
        start = pl.multiple_of(j * tk, tk)
        k = k_ref[pl.ds(start, tk), :]
        q = q_ref[...]
        zero = jnp.zeros_like(q)
        z_ref[0] = _dot_nt(jnp.where(low, q, zero), k)
        z_ref[1] = _dot_nt(jnp.where(low, zero, q), k)

    def absorb(j, z_ref, masked):
        start = pl.multiple_of(j * tk, tk)
        v = v_ref[pl.ds(start, tk), :]
        tri = tri_ref[...]
        if masked:
            diff = lax.broadcasted_iota(jnp.int32, (tq, tk), 1) - lax.broadcasted_iota(jnp.int32, (tq, tk), 0)
            keep = diff < i * tq - start
        staged = []
        for hh in range(2):
            z = z_ref[hh]
            neg_abs = lax.bitcast_convert_type(lax.bitcast_convert_type(z, jnp.uint32) | sign, F32)
            sp = jnp.maximum(z, 0.0) + jnp.log2(1.0 + jnp.exp2(neg_abs))
            if masked:
                sp = jnp.where(keep, sp, 0.0)
            c = c_ref[hh]
            parts = []
            for cc in reversed(range(tk // tc)):
                cs = slice(cc * tc, (cc + 1) * tc)
                spc = sp[:, cs]
                right = _dot(spc.astype(BF16), tri)
                parts.append((cs, z[:, cs] - spc, right, c))
                c = c + right[:, 0:1] + spc[:, 0:1]
            c_ref[hh] = c
            staged.append(parts)
        for hh in range(2):
            acc = acc_ref[hh]
            for cs, log_beta, right, c in staged[hh]:
                a = jnp.exp2(log_beta - right - c)
                if masked:
                    a = jnp.where(keep[:, cs], a, 0.0)
                acc = acc + _dot(a.astype(BF16), v[cs, :])
            acc_ref[hh] = acc

    def pairs(first_ref, second_ref):
        npair = n_full // 2

        def body(t, carry):
            top = 2 * (npair - t) - 1
            logits(top - 1, second_ref)
            absorb(top, first_ref, False)
            logits(jnp.maximum(top - 2, 0), first_ref)
            absorb(top - 1, second_ref, False)
            return carry

        lax.fori_loop(0, npair, body, 0)

    logits(n_full, za_ref)

    @pl.when(n_full % 2 == 1)
    def _():
        logits(n_full - 1, zb_ref)
        absorb(n_full, za_ref, True)
        logits(jnp.maximum(n_full - 2, 0), za_ref)
        absorb(n_full - 1, zb_ref, False)
        pairs(za_ref, zb_ref)

    @pl.when(n_full % 2 == 0)
    def _():
        logits(jnp.maximum(n_full - 1, 0), zb_ref)
        absorb(n_full, za_ref, True)
        pairs(zb_ref, za_ref)

    o_ref[...] = jnp.where(low, acc_ref[0], acc_ref[1])


def _sb_attention(q, k, v, tri, batch, seq):
    tq, tk, tc = TQ_ATTN, TK_ATTN, TC_SB
    q = q.reshape(batch, seq, -1)
    k = k.reshape(batch, seq, -1)
    v = v.reshape(batch, seq, -1)
    out = pl.pallas_call(
        functools.partial(_sb_kernel, tq=tq, tk=tk, tc=tc),
        grid=(batch, SB_HEADS // 2, seq // tq),
        in_specs=[pl.BlockSpec((None, tq, LANES), lambda b, p, i: (b, i, p)),
                  pl.BlockSpec((None, seq, LANES), lambda b, p, i: (b, 0, p)),
                  pl.BlockSpec((None, seq, LANES), lambda b, p, i: (b, 0, p)),
                  pl.BlockSpec((tc, tc), lambda b, p, i: (0, 0))],
        out_specs=pl.BlockSpec((None, tq, LANES), lambda b, p, i: (b, i, p)),
        out_shape=jax.ShapeDtypeStruct((batch, seq, WIDTH_C), F32),
        scratch_shapes=[pltpu.VMEM((2, tq, tk), F32), pltpu.VMEM((2, tq, tk), F32),
                        pltpu.VMEM((2, tq, 1), F32), pltpu.VMEM((2, tq, LANES), F32)],
        compiler_params=pltpu.CompilerParams(dimension_semantics=("parallel", "parallel", "arbitrary"),
                                             vmem_limit_bytes=VMEM_LIMIT),
        name="sb_attention",
    )(q, k, v, tri)
    return out.reshape(batch * seq, WIDTH_C)


def _swa_kernel(slopes_ref, sinks_ref, q_ref, k_ref, v_ref, o_ref, *, tq):
    w = SWA_WINDOW
    group = SWA_Q_HEADS // SWA_KV_HEADS
    hkv = pl.program_id(1)
    i = pl.program_id(2)
    lane = lax.broadcasted_iota(jnp.int32, (1, LANES), 1)
    low = lane < HEAD_DIM
    row = lax.broadcasted_iota(jnp.int32, (w, 2 * w), 0)
    col = lax.broadcasted_iota(jnp.int32, (w, 2 * w), 1)
    for n in range(tq // w):
        q0 = i * tq + n * w
        ks = pl.multiple_of(jnp.maximum(q0 - w, 0), w)
        k = k_ref[pl.ds(ks, 2 * w), :]
        v = v_ref[pl.ds(ks, 2 * w), :]
        dist = (q0 - ks) + row - col
        valid = (dist >= 0) & (dist < w)
        distf = dist.astype(F32)
        for pr in range(group // 2):
            qp = q_ref[n * w:(n + 1) * w, LANES * pr:LANES * (pr + 1)]
            zero = jnp.zeros_like(qp)
            outs = []
            for half in range(2):
                head = hkv * group + pr * 2 + half
                qh = jnp.where(low, qp, zero) if half == 0 else jnp.where(low, zero, qp)
                s = _dot_nt(qh, k) - slopes_ref[head] * distf
                s = jnp.where(valid, s, NEG)
                sink = sinks_ref[head]
                m = jnp.maximum(jnp.max(s, axis=-1, keepdims=True), sink)
                p = jnp.exp(s - m)
                den = jnp.sum(p, axis=-1, keepdims=True) + jnp.exp(sink - m)
                outs.append(_dot(p.astype(BF16), v) / den)
            o_ref[n * w:(n + 1) * w, LANES * pr:LANES * (pr + 1)] = jnp.where(low, outs[0], outs[1])


def _swa_attention(q, k, v, slopes, sinks, batch, seq):
    tq = TQ_SWA
    q = q.reshape(batch, seq, -1)
    k = k.reshape(batch, seq, -1)
    v = v.reshape(batch, seq, -1)
    smem = pl.BlockSpec(memory_space=pltpu.SMEM)
    out = pl.pallas_call(
        functools.partial(_swa_kernel, tq=tq),
        grid=(batch, SWA_KV_HEADS, seq // tq),
        in_specs=[smem, smem,
                  pl.BlockSpec((None, tq, 2 * LANES), lambda b, h, i: (b, i, h)),
                  pl.BlockSpec((None, seq, LANES), lambda b, h, i: (b, 0, h)),
                  pl.BlockSpec((None, seq, LANES), lambda b, h, i: (b, 0, h))],
        out_specs=pl.BlockSpec((None, tq, 2 * LANES), lambda b, h, i: (b, i, h)),
        out_shape=jax.ShapeDtypeStruct((batch, seq, WIDTH_B), F32),
        compiler_params=pltpu.CompilerParams(dimension_semantics=("parallel", "parallel", "arbitrary"),
                                             vmem_limit_bytes=VMEM_LIMIT),
        name="swa_attention",
    )(slopes, sinks, q, k, v)
    return out.reshape(batch * seq, WIDTH_B)


def _out_proj_kernel(*refs, moe):
    if moe:
        oa, ob, oc, x_ref, gout, wout, gffn, wr_hi, wr_lo, xo_ref, h_ref, sel_ref = refs
    else:
        oa, ob, oc, x_ref, gout, wout, gffn, xo_ref, h_ref = refs
    g = gout[...]
    b0, b1 = WIDTH_A, WIDTH_A + WIDTH_B
    na = _rms(oa[...], g[:, :b0]).astype(BF16)
    nb = _rms(ob[...], g[:, b0:b1]).astype(BF16)
    nc = _rms(oc[...], g[:, b1:]).astype(BF16)
    xn = x_ref[...] + _dot(na, wout[:b0, :]) + _dot(nb, wout[b0:b1, :]) + _dot(nc, wout[b1:, :])
    xo_ref[...] = xn
    hf = _rms(xn, gffn[...])
    hb = hf.astype(BF16)
    h_ref[...] = hf if moe else hb
    if moe:
        lo = (hf - hb.astype(F32)).astype(BF16)
        logits = _dot(hb, wr_hi[...]) + _dot(lo, wr_hi[...]) + _dot(hb, wr_lo[...])
        lane = lax.broadcasted_iota(jnp.int32, logits.shape, 1)
        lg = jnp.where(lane < N_EXPERTS, logits, -jnp.inf)
        m1 = jnp.max(lg, axis=-1, keepdims=True)
        i1 = jnp.min(jnp.where(lg == m1, lane, LANES), axis=-1, keepdims=True)
        lg2 = jnp.where(lane == i1, -jnp.inf, lg)
        m2 = jnp.max(lg2, axis=-1, keepdims=True)
        i2 = jnp.min(jnp.where(lg2 == m2, lane, LANES), axis=-1, keepdims=True)
        e = jnp.exp(m2 - m1)
        g1 = 1.0 / (1.0 + e)
        g2 = e / (1.0 + e)
        sel_ref[...] = jnp.where(lane == 0, i1.astype(F32),
                                 jnp.where(lane == 1, i2.astype(F32),
                                           jnp.where(lane == 2, g1, jnp.where(lane == 3, g2, 0.0))))


def _out_proj(oa, ob, oc, x, lw, moe):
    n = x.shape[0]
    tm = TM_PROJ
    row = lambda i: (i, 0)
    fixed = lambda i: (0, 0)
    full = lambda a: pl.BlockSpec(a.shape, fixed)
    args = [oa, ob, oc, x, lw["out_g"], lw["w_out"], lw["ffn_g"]]
    in_specs = [pl.BlockSpec((tm, WIDTH_A), row), pl.BlockSpec((tm, WIDTH_B), row), pl.BlockSpec((tm, WIDTH_C), row),
                pl.BlockSpec((tm, D_MODEL), row), full(lw["out_g"]), full(lw["w_out"]), full(lw["ffn_g"])]
    out_specs = [pl.BlockSpec((tm, D_MODEL), row), pl.BlockSpec((tm, D_MODEL), row)]
    out_shape = [jax.ShapeDtypeStruct((n, D_MODEL), F32), jax.ShapeDtypeStruct((n, D_MODEL), F32 if moe else BF16)]
    if moe:
        args += [lw["router_hi"], lw["router_lo"]]
        in_specs += [full(lw["router_hi"]), full(lw["router_lo"])]
        out_specs.append(pl.BlockSpec((tm, LANES), row))
        out_shape.append(jax.ShapeDtypeStruct((n, LANES), F32))
    return pl.pallas_call(
        functools.partial(_out_proj_kernel, moe=moe),
        grid=(n // tm,),
        in_specs=in_specs, out_specs=out_specs, out_shape=out_shape,
        compiler_params=pltpu.CompilerParams(dimension_semantics=("parallel",), vmem_limit_bytes=VMEM_LIMIT),
        name="out_proj_moe" if moe else "out_proj",
    )(*args)


def _swiglu_partial(h, wg, wu, wd):
    h = h.astype(BF16)
    g = _dot(h, wg)
    u = _dot(h, wu)
    return _dot((g * jax.nn.sigmoid(g) * u).astype(BF16), wd)


def _ffn_kernel(h_ref, wg_ref, wu_ref, wd_ref, x_ref, o_ref):
    j = pl.program_id(1)
    y = _swiglu_partial(h_ref[...], wg_ref[...], wu_ref[...], wd_ref[...])

    @pl.when(j == 0)
    def _():
        o_ref[...] = x_ref[...] + y

    @pl.when(j > 0)
    def _():
        o_ref[...] += y


def _dense_ffn(h, x, lw):
    n = x.shape[0]
    tm, tf = TM_FFN, TF_FFN
    return pl.pallas_call(
        _ffn_kernel,
        grid=(n // tm, D_FF // tf),
        in_specs=[pl.BlockSpec((tm, D_MODEL), lambda i, j: (i, 0)),
                  pl.BlockSpec((D_MODEL, tf), lambda i, j: (0, j)),
                  pl.BlockSpec((D_MODEL, tf), lambda i, j: (0, j)),
                  pl.BlockSpec((tf, D_MODEL), lambda i, j: (j, 0)),
                  pl.BlockSpec((tm, D_MODEL), lambda i, j: (i, 0))],
        out_specs=pl.BlockSpec((tm, D_MODEL), lambda i, j: (i, 0)),
        out_shape=jax.ShapeDtypeStruct((n, D_MODEL), F32),
        compiler_params=pltpu.CompilerParams(dimension_semantics=("parallel", "arbitrary"),
                                             vmem_limit_bytes=VMEM_LIMIT),
        name="dense_ffn",
    )(h, lw["w_gate"], lw["w_up"], lw["w_down"], x)


def _moe_kernel(blk_e_ref, n_used_ref, xs_ref, wg_ref, wu_ref, wd_ref, o_ref):
    i = pl.program_id(0)
    j = pl.program_id(1)
    used = i < n_used_ref[0]

    @pl.when(used)
    def _():
        y = _swiglu_partial(xs_ref[...], wg_ref[...], wu_ref[...], wd_ref[...])

        @pl.when(j == 0)
        def _():
            o_ref[...] = y

        @pl.when(j > 0)
        def _():
            o_ref[...] += y

    @pl.when(jnp.logical_not(used) & (j == 0))
    def _():
        o_ref[...] = jnp.zeros_like(o_ref)


def _moe_experts(xs, blk_e, n_used, lw):
    p = xs.shape[0]
    tm, tf = TM_MOE, TF_MOE
    n_ff = D_FF_EXPERT // tf

    def col(i, j):
        return jnp.where(i % 2 == 0, j, n_ff - 1 - j)

    grid_spec = pltpu.PrefetchScalarGridSpec(
        num_scalar_prefetch=2,
        grid=(p // tm, n_ff),
        in_specs=[pl.BlockSpec((tm, D_MODEL), lambda i, j, be, nu: (i, 0)),
                  pl.BlockSpec((None, D_MODEL, tf), lambda i, j, be, nu: (be[i], 0, col(i, j))),
                  pl.BlockSpec((None, D_MODEL, tf), lambda i, j, be, nu: (be[i], 0, col(i, j))),
                  pl.BlockSpec((None, tf, D_MODEL), lambda i, j, be, nu: (be[i], col(i, j), 0))],
        out_specs=pl.BlockSpec((tm, D_MODEL), lambda i, j, be, nu: (i, 0)),
    )
    return pl.pallas_call(
        _moe_kernel,
        grid_spec=grid_spec,
        out_shape=jax.ShapeDtypeStruct((p, D_MODEL), F32),
        compiler_params=pltpu.CompilerParams(dimension_semantics=("arbitrary", "arbitrary"),
                                             vmem_limit_bytes=VMEM_LIMIT),
        name="moe_experts",
    )(blk_e, n_used, xs, lw["w_gate"], lw["w_up"], lw["w_down"])


def _row_copy(src_ref, src_row, dst_ref, dst_row, sem):
    return pltpu.make_async_copy(src_ref.at[pl.ds(src_row, 1)], dst_ref.at[pl.ds(dst_row, 1)], sem)


def _dispatch_kernel(dest_ref, h_ref, xs_in_ref, xs_ref, sem, *, tm):
    del xs_in_ref

    def start(r, carry):
        for k in range(2):
            _row_copy(h_ref, r, xs_ref, dest_ref[0, 2 * r + k], sem).start()
        return carry

    def wait(r, carry):
        for k in range(2):
            _row_copy(h_ref, r, xs_ref, dest_ref[0, 2 * r + k], sem).wait()
        return carry

    lax.fori_loop(0, tm, start, 0, unroll=8)
    lax.fori_loop(0, tm, wait, 0, unroll=8)


def _dispatch(dest_blk, h, p):
    n = h.shape[0]
    tm = TM_ROUTE
    xs0 = jnp.zeros((p, D_MODEL), F32)
    return pl.pallas_call(
        functools.partial(_dispatch_kernel, tm=tm),
        grid=(n // tm,),
        in_specs=[pl.BlockSpec((None, 1, 2 * tm), lambda i: (i, 0, 0), memory_space=pltpu.SMEM),
                  pl.BlockSpec((tm, D_MODEL), lambda i: (i, 0)),
                  pl.BlockSpec(memory_space=pl.ANY)],
        out_specs=pl.BlockSpec(memory_space=pl.ANY),
        out_shape=jax.ShapeDtypeStruct((p, D_MODEL), F32),
        scratch_shapes=[pltpu.SemaphoreType.DMA(())],
        input_output_aliases={2: 0},
        compiler_params=pltpu.CompilerParams(dimension_semantics=("arbitrary",), vmem_limit_bytes=VMEM_LIMIT,
                                             has_side_effects=True),
        name="moe_dispatch",
    )(dest_blk, h, xs0)


def _combine_kernel(dest_ref, ys_ref, x_ref, sel_ref, o_ref, ybuf, sem, *, tm):
    def start(r, carry):
        for k in range(2):
            _row_copy(ys_ref, dest_ref[0, 2 * r + k], ybuf.at[k], r, sem).start()
        return carry

    def wait(r, carry):
        for k in range(2):
            _row_copy(ys_ref, dest_ref[0, 2 * r + k], ybuf.at[k], r, sem).wait()
        return carry

    lax.fori_loop(0, tm, start, 0, unroll=8)
    lax.fori_loop(0, tm, wait, 0, unroll=8)
    sel = sel_ref[...]
    o_ref[...] = x_ref[...] + ybuf[0] * sel[:, 2:3] + ybuf[1] * sel[:, 3:4]


def _combine(dest_blk, ys, x, sel):
    n = x.shape[0]
    tm = TM_ROUTE
    return pl.pallas_call(
        functools.partial(_combine_kernel, tm=tm),
        grid=(n // tm,),
        in_specs=[pl.BlockSpec((None, 1, 2 * tm), lambda i: (i, 0, 0), memory_space=pltpu.SMEM),
                  pl.BlockSpec(memory_space=pl.ANY),
                  pl.BlockSpec((tm, D_MODEL), lambda i: (i, 0)),
                  pl.BlockSpec((tm, LANES), lambda i: (i, 0))],
        out_specs=pl.BlockSpec((tm, D_MODEL), lambda i: (i, 0)),
        out_shape=jax.ShapeDtypeStruct((n, D_MODEL), F32),
        scratch_shapes=[pltpu.VMEM((2, tm, D_MODEL), F32), pltpu.SemaphoreType.DMA(())],
        compiler_params=pltpu.CompilerParams(dimension_semantics=("arbitrary",), vmem_limit_bytes=VMEM_LIMIT),
        name="moe_combine",
    )(dest_blk, ys, x, sel)


def _moe_ffn(h, x, sel, lw):
    n = x.shape[0]
    tm = TM_MOE
    top_idx = sel[:, :2].astype(jnp.int32)
    flat_e = top_idx.reshape(-1)
    onehot = (flat_e[:, None] == jnp.arange(N_EXPERTS, dtype=jnp.int32)[None, :]).astype(jnp.int32)
    csum = jnp.cumsum(onehot, axis=0)
    rank = jnp.sum(csum * onehot, axis=1) - 1
    counts = csum[-1]
    padded = (counts + tm - 1) // tm * tm
    pend = jnp.cumsum(padded)
    pstart = pend - padded
    dest = pstart[flat_e] + rank
    p = n * 2 + N_EXPERTS * tm
    n_blk = p // tm
    blk_start = jnp.arange(n_blk, dtype=jnp.int32) * tm
    blk_e = jnp.minimum(jnp.sum((pend[None, :] <= blk_start[:, None]).astype(jnp.int32), axis=1), N_EXPERTS - 1)
    n_used = (pend[-1:] // tm).astype(jnp.int32)
    dest_blk = dest.astype(jnp.int32).reshape(n // TM_ROUTE, 1, 2 * TM_ROUTE)
    xs = _dispatch(dest_blk, h, p)
    ys = _moe_experts(xs, blk_e, n_used, lw)
    return _combine(dest_blk, ys, x, sel)


def _pad_cols(w, width):
    return jnp.pad(w, ((0, 0), (0, width - w.shape[1])))


def _head_groups(w, heads, dim):
    kdim = w.shape[0]
    return jnp.pad(w.reshape(kdim, heads, dim), ((0, 0), (0, 0), (0, LANES - dim))).reshape(kdim, heads * LANES)


def _layer_weights(layer, p):
    w_in = p["w_in"][layer]
    a, b, c = w_in[:, :COLS_A], w_in[:, COLS_A:COLS_A + COLS_B], w_in[:, COLS_A + COLS_B:]
    kpe = jnp.pad(a[:, MLA_Q_RANK + MLA_KV_RANK:], ((0, 0), (MLA_NOPE, LANES - MLA_QK)))
    bq = b[:, :WIDTH_B]

    def dup(w):
        return jnp.repeat(w.reshape(D_MODEL, SWA_KV_HEADS, 1, HEAD_DIM), 2, axis=2).reshape(D_MODEL, 2 * LANES)

    bk = dup(b[:, WIDTH_B:WIDTH_B + SWA_KV_HEADS * HEAD_DIM])
    bv = dup(b[:, WIDTH_B + SWA_KV_HEADS * HEAD_DIM:])
    w_in_p = jnp.concatenate([a[:, :MLA_Q_RANK + MLA_KV_RANK], kpe, bq, bk, bv, c], axis=1).astype(BF16)
    w_ukv = p["mla_w_ukv"][layer].reshape(MLA_KV_RANK, MLA_HEADS, MLA_NOPE + MLA_V)
    w_uk = _head_groups(w_ukv[:, :, :MLA_NOPE].reshape(MLA_KV_RANK, -1), MLA_HEADS, MLA_NOPE)
    w_uv = w_ukv[:, :, MLA_NOPE:].reshape(MLA_KV_RANK, -1)
    row = lambda g: g.reshape(1, -1).astype(F32)
    lw = {
        "attn_g": row(p["attn_norm_g"][layer]),
        "w_in": w_in_p,
        "cq_g": row(p["mla_cq_norm_g"][layer]),
        "w_uq": _head_groups(p["mla_w_uq"][layer], MLA_HEADS, MLA_QK).astype(BF16),
        "ckv_g": row(p["mla_ckv_norm_g"][layer]),
        "w_ukv": jnp.concatenate([w_uk, w_uv], axis=1).astype(BF16),
        "mla_qg": _pad_cols(row(p["mla_qn_g"][layer]), LANES),
        "mla_kg": _pad_cols(row(p["mla_kn_g"][layer]), LANES),
        "swa_qg": jnp.tile(row(p["swa_qn_g"][layer]), (1, 2)),
        "swa_kg": jnp.tile(row(p["swa_kn_g"][layer]), (1, 2)),
        "sinks": p["swa_sinks"][layer].astype(F32),
        "out_g": row(p["out_norm_g"][layer]),
        "w_out": p["w_out"][layer].astype(BF16),
        "ffn_g": row(p["ffn_norm_g"][layer]),
    }
    j = layer // 2
    if layer % 2 == 0:
        lw["w_gate"] = p["dense_w_gate"][j].astype(BF16)
        lw["w_up"] = p["dense_w_up"][j].astype(BF16)
        lw["w_down"] = p["dense_w_down"][j].astype(BF16)
    else:
        rw = _pad_cols(p["router_w"][j], LANES)
        hi = rw.astype(BF16)
        lw["router_hi"] = hi
        lw["router_lo"] = (rw - hi.astype(F32)).astype(BF16)
        lw["w_gate"] = p["moe_w_gate"][j].astype(BF16)
        lw["w_up"] = p["moe_w_up"][j].astype(BF16)
        lw["w_down"] = p["moe_w_down"][j].astype(BF16)
    return lw


def _rope_tables(seq):
    half = MLA_ROPE // 2
    inv = ROPE_THETA ** (-jnp.arange(half, dtype=F32) / half)
    ang = jnp.arange(seq).astype(F32)[:, None] * inv[None, :]
    cos, sin = jnp.cos(ang), jnp.sin(ang)
    zl = jnp.zeros((seq, MLA_NOPE), F32)
    zh = jnp.zeros((seq, half), F32)
    zr = jnp.zeros((seq, LANES - MLA_QK), F32)
    cos_t = jnp.concatenate([zl + 1.0, cos, cos, zr + 1.0], axis=1)
    sina = jnp.concatenate([zl, -sin, zh, zr], axis=1)
    sinb = jnp.concatenate([zl, zh, sin, zr], axis=1)
    return cos_t, sina, sinb


def kernel(x, attn_norm_g, w_in, mla_cq_norm_g, mla_w_uq, mla_ckv_norm_g, mla_w_ukv, mla_qn_g, mla_kn_g, swa_qn_g, swa_kn_g, swa_sinks, out_norm_g, w_out, ffn_norm_g, dense_w_gate, dense_w_up, dense_w_down, router_w, moe_w_gate, moe_w_up, moe_w_down):
    params = dict(attn_norm_g=attn_norm_g, w_in=w_in, mla_cq_norm_g=mla_cq_norm_g, mla_w_uq=mla_w_uq,
                  mla_ckv_norm_g=mla_ckv_norm_g, mla_w_ukv=mla_w_ukv, mla_qn_g=mla_qn_g, mla_kn_g=mla_kn_g,
                  swa_qn_g=swa_qn_g, swa_kn_g=swa_kn_g, swa_sinks=swa_sinks, out_norm_g=out_norm_g, w_out=w_out,
                  ffn_norm_g=ffn_norm_g, dense_w_gate=dense_w_gate, dense_w_up=dense_w_up,
                  dense_w_down=dense_w_down, router_w=router_w, moe_w_gate=moe_w_gate, moe_w_up=moe_w_up,
                  moe_w_down=moe_w_down)
    batch, seq, _ = x.shape
    depth = w_in.shape[0]
    tabs = _rope_tables(seq)
    slopes = jnp.exp2(-8.0 * jnp.arange(1, SWA_Q_HEADS + 1, dtype=F32) / SWA_Q_HEADS)
    tri = (jnp.arange(TC_SB)[:, None] > jnp.arange(TC_SB)[None, :]).astype(BF16)
    xf =x.reshape(batch * seq, D_MODEL)
    for layer in range(depth):
        lw = _layer_weights(layer, params)
        mq, mk, mv, sq, sk, sv, cq, ck, cv = _in_proj(xf, lw, tabs, seq)
        o_a = _mla_attention(mq, mk, mv, batch, seq)
        o_b = _swa_attention(sq, sk, sv, slopes, lw["sinks"], batch, seq)
        o_c = _sb_attention(cq, ck, cv, tri, batch, seq)
        if layer % 2 == 0:
            xf, h = _out_proj(o_a, o_b, o_c, xf, lw, moe=False)
            xf = _dense_ffn(h, xf, lw)
        else:
            xf, h, sel = _out_proj(o_a, o_b, o_c, xf, lw, moe=True)
            xf = _moe_ffn(h, xf, sel, lw)
    return xf.reshape(batch, seq, D_MODEL)
```

```python
import functools

import jax
import jax.numpy as jnp
from jax import lax
from jax.experimental import pallas as pl
from jax.experimental.pallas import tpu as pltpu

F32 = jnp.float32
BF16 = jnp.bfloat16

D_MODEL = 1024
HEAD_DIM = 64
MLA_HEADS = 4
MLA_Q_RANK = 256
MLA_KV_RANK = 128
MLA_NOPE = 64
MLA_ROPE = 32
MLA_QK = MLA_NOPE + MLA_ROPE
MLA_V = 64
ROPE_THETA = 10000.0
SWA_Q_HEADS = 8
SWA_KV_HEADS = 2
SWA_WINDOW = 128
SB_HEADS = 4
WIDTH_A = MLA_HEADS * MLA_V
WIDTH_B = SWA_Q_HEADS * HEAD_DIM
WIDTH_C = SB_HEADS * HEAD_DIM
COLS_A = MLA_Q_RANK + MLA_KV_RANK + MLA_ROPE
COLS_B = (SWA_Q_HEADS + 2 * SWA_KV_HEADS) * HEAD_DIM
D_FF = 2816
N_EXPERTS = 8
D_FF_EXPERT = 3584
RMS_EPS = 1e-6
NEG = -1e30
LOG2E = 1.4426950408889634

LANES = 128
VMEM_LIMIT = 56 * 1024 * 1024

TM_PROJ = 512
TQ_ATTN = 256
TK_ATTN = 1024
TC_SB = 256
SB_DEAD_BITS = 151.0
TQ_SWA = 512
TM_FFN = 512
TF_FFN = 1408
TM_MOE = 512
TM_ROUTE = 512
TF_MOE = 1792

P_CQ = 0
P_CKV = 256
P_KPE = 384
P_SWQ = 512
P_SWK = 1024
P_SWV = 1280
P_SBQ = 1536
P_SBK = 1792
P_SBV = 2048
P_COLS = 2304


def _dot(a, b):
    return jnp.dot(a, b, preferred_element_type=F32)


def _dot_nt(a, b):
    return lax.dot_general(a, b, (((1,), (1,)), ((), ())), preferred_element_type=F32)


def _rms(x, g):
    return x * lax.rsqrt(jnp.mean(x * x, axis=-1, keepdims=True) + RMS_EPS) * g


def _in_proj_kernel(x_ref, g_ref, w_ref, gcq_ref, wuq_ref, gckv_ref, wukv_ref, gq_ref, gk_ref,
                    cos_ref, sina_ref, sinb_ref, sgq_ref, sgk_ref,
                    mq_ref, mk_ref, mv_ref, sq_ref, sk_ref, sv_ref, cq_ref, ck_ref, cv_ref):
    h = _rms(x_ref[...], g_ref[...]).astype(BF16)

    def proj(lo, hi):
        return _dot(h, w_ref[:, lo:hi])

    cos = cos_ref[...]
    sina = sina_ref[...]
    sinb = sinb_ref[...]

    def rope(t):
        return t * cos + pltpu.roll(t, LANES - MLA_ROPE // 2, 1) * sina + pltpu.roll(t, MLA_ROPE // 2, 1) * sinb

    def head_norm(t, g):
        return t * lax.rsqrt(jnp.sum(t * t, axis=-1, keepdims=True) * (1.0 / MLA_QK) + RMS_EPS) * g

    lane = lax.broadcasted_iota(jnp.int32, (1, LANES), 1)
    low = lane < HEAD_DIM

    def pair_norm(t, g):
        t2 = t * t
        lo = jnp.sum(jnp.where(low, t2, 0.0), axis=-1, keepdims=True)
        hi = jnp.sum(jnp.where(low, 0.0, t2), axis=-1, keepdims=True)
        ms = jnp.where(low, lo, hi) * (1.0 / HEAD_DIM)
        return t * lax.rsqrt(ms + RMS_EPS) * g

    scale = HEAD_DIM ** -0.5
    a = proj(P_CQ, P_SWQ)
    q = _dot(_rms(a[:, :MLA_Q_RANK], gcq_ref[...]).astype(BF16), wuq_ref[...])
    for hh in range(MLA_HEADS):
        sl = slice(LANES * hh, LANES * (hh + 1))
        mq_ref[:, sl] = (rope(head_norm(q[:, sl], gq_ref[...])) * (MLA_QK ** -0.5 * LOG2E)).astype(BF16)
    kv = _dot(_rms(a[:, P_CKV:P_KPE], gckv_ref[...]).astype(BF16), wukv_ref[...])
    kpe = a[:, P_KPE:P_SWQ]
    for hh in range(MLA_HEADS):
        sl = slice(LANES * hh, LANES * (hh + 1))
        mk_ref[:, sl] = rope(head_norm(kv[:, sl] + kpe, gk_ref[...])).astype(BF16)
    mv_ref[...] = kv[:, MLA_HEADS * LANES:].astype(BF16)
    bq = proj(P_SWQ, P_SWK)
    for gi in range(WIDTH_B // LANES):
        sl = slice(LANES * gi, LANES * (gi + 1))
        sq_ref[:, sl] = (pair_norm(bq[:, sl], sgq_ref[...]) * scale).astype(BF16)
    bk = proj(P_SWK, P_SWV)
    for gi in range(SWA_KV_HEADS):
        sl = slice(LANES * gi, LANES * (gi + 1))
        sk_ref[:, sl] = pair_norm(bk[:, sl], sgk_ref[...]).astype(BF16)
    sv_ref[...] = proj(P_SWV, P_SBQ).astype(BF16)
    cq_ref[...] = (proj(P_SBQ, P_SBK) * (scale * LOG2E)).astype(BF16)
    ck_ref[...] = proj(P_SBK, P_SBV).astype(BF16)
    cv_ref[...] = proj(P_SBV, P_COLS).astype(BF16)


def _in_proj(x, lw, tabs, seq):
    n = x.shape[0]
    tm = TM_PROJ
    nt = seq // tm
    row = lambda i: (i, 0)
    fixed = lambda i: (0, 0)
    pos = lambda i: (i % nt, 0)
    full = lambda a: pl.BlockSpec(a.shape, fixed)
    cos, sina, sinb = tabs
    widths = (4 * LANES, 4 * LANES, WIDTH_A, WIDTH_B, 2 * LANES, 2 * LANES, WIDTH_C, WIDTH_C, WIDTH_C)
    return pl.pallas_call(
        _in_proj_kernel,
        grid=(n // tm,),
        in_specs=[pl.BlockSpec((tm, D_MODEL), row), full(lw["attn_g"]), full(lw["w_in"]), full(lw["cq_g"]),
                  full(lw["w_uq"]), full(lw["ckv_g"]), full(lw["w_ukv"]), full(lw["mla_qg"]), full(lw["mla_kg"]),
                  pl.BlockSpec((tm, LANES), pos), pl.BlockSpec((tm, LANES), pos), pl.BlockSpec((tm, LANES), pos),
                  full(lw["swa_qg"]), full(lw["swa_kg"])],
        out_specs=[pl.BlockSpec((tm, w), row) for w in widths],
        out_shape=[jax.ShapeDtypeStruct((n, w), BF16) for w in widths],
        compiler_params=pltpu.CompilerParams(dimension_semantics=("parallel",), vmem_limit_bytes=VMEM_LIMIT),
        name="in_proj",
    )(x, lw["attn_g"], lw["w_in"], lw["cq_g"], lw["w_uq"], lw["ckv_g"], lw["w_ukv"], lw["mla_qg"], lw["mla_kg"],
      cos, sina, sinb, lw["swa_qg"], lw["swa_kg"])


def _mla_kernel(q_ref, k_ref, v_ref, o_ref, sa_ref, sb_ref, m_ref, l_ref, acc_ref, *, tq, tk):
    i = pl.program_id(2)
    n_full = (i * tq) // tk
    m_ref[...] = jnp.full(m_ref.shape, NEG, F32)
    l_ref[...] = jnp.zeros(l_ref.shape, F32)
    acc_ref[...] = jnp.zeros(acc_ref.shape, F32)

    def scores(j, s_ref):
        start = pl.multiple_of(j * tk, tk)
        k = k_ref[pl.ds(start, tk), :]
        for hh in range(2):
            sl = slice(LANES * hh, LANES * (hh + 1))
            s_ref[hh] = _dot_nt(q_ref[:, sl], k[:, sl])

    def absorb(j, s_ref, masked, width=tk):
        start = pl.multiple_of(j * tk, tk)
        v = v_ref[pl.ds(start, width), :]
        if masked:
            diff = lax.broadcasted_iota(jnp.int32, (tq, width), 1) - lax.broadcasted_iota(jnp.int32, (tq, width), 0)
            keep = diff <= i * tq - start
        for hh in range(2):
            s = s_ref[hh, :, :width]
            if masked:
                s = jnp.where(keep, s, NEG)
            m = m_ref[hh]
            m_new = jnp.maximum(m, jnp.max(s, axis=-1, keepdims=True))
            alpha = jnp.exp2(m - m_new)
            p = jnp.exp2(s - m_new)
            l_ref[hh] = alpha * l_ref[hh] + jnp.sum(p, axis=-1, keepdims=True)
            acc_ref[hh] = alpha * acc_ref[hh] + _dot(p.astype(BF16), v)
            m_ref[hh] = m_new

    scores(0, sa_ref)

    def pair(t, carry):
        scores(2 * t + 1, sb_ref)
        absorb(2 * t, sa_ref, False)
        scores(2 * t + 2, sa_ref)
        absorb(2 * t + 1, sb_ref, False)
        return carry

    lax.fori_loop(0, n_full // 2, pair, 0)
    base = 2 * (n_full // 2)

    def absorb_diagonal(s_ref):
        blocks = (i * tq - n_full * tk) // tq + 1
        for nb in range(1, tk // tq + 1):
            pl.when(blocks == nb)(functools.partial(absorb, n_full, s_ref, True, nb * tq))

    @pl.when(n_full % 2 == 1)
    def _():
        scores(base + 1, sb_ref)
        absorb(base, sa_ref, False)
        absorb_diagonal(sb_ref)

    @pl.when(n_full % 2 == 0)
    def _():
        absorb_diagonal(sa_ref)

    lane = lax.broadcasted_iota(jnp.int32, (1, LANES), 1)
    o_ref[...] = jnp.where(lane < MLA_V, acc_ref[0] / l_ref[0], acc_ref[1] / l_ref[1])


def _mla_attention(q, k, v, batch, seq):
    tq, tk = TQ_ATTN, TK_ATTN
    q = q.reshape(batch, seq, -1)
    k = k.reshape(batch, seq, -1)
    v = v.reshape(batch, seq, -1)
    out = pl.pallas_call(
        functools.partial(_mla_kernel, tq=tq, tk=tk),
        grid=(batch, MLA_HEADS // 2, seq // tq),
        in_specs=[pl.BlockSpec((None, tq, 2 * LANES), lambda b, p, i: (b, i, p)),
                  pl.BlockSpec((None, seq, 2 * LANES), lambda b, p, i: (b, 0, p)),
                  pl.BlockSpec((None, seq, LANES), lambda b, p, i: (b, 0, p))],
        out_specs=pl.BlockSpec((None, tq, LANES), lambda b, p, i: (b, i, p)),
        out_shape=jax.ShapeDtypeStruct((batch, seq, WIDTH_A), F32),
        scratch_shapes=[pltpu.VMEM((2, tq, tk), F32), pltpu.VMEM((2, tq, tk), F32),
                        pltpu.VMEM((2, tq, 1), F32), pltpu.VMEM((2, tq, 1), F32), pltpu.VMEM((2, tq, LANES), F32)],
        compiler_params=pltpu.CompilerParams(dimension_semantics=("parallel", "parallel", "arbitrary"),
                                             vmem_limit_bytes=VMEM_LIMIT),
        name="mla_attention",
    )(q, k, v)
    return out.reshape(batch * seq, WIDTH_A)


def _sb_kernel(q_ref, k_ref, v_ref, tri_ref, o_ref, za_ref, zb_ref, c_ref, acc_ref, *, tq, tk, tc):
    i = pl.program_id(2)
    n_full = (i * tq) // tk
    lane = lax.broadcasted_iota(jnp.int32, (1, LANES), 1)
    low = lane < HEAD_DIM
    c_ref[...] = jnp.zeros(c_ref.shape, F32)
    acc_ref[...] = jnp.zeros(acc_ref.shape, F32)
    sign = jnp.uint32(0x80000000)

    def logits(j, z_ref):
        start = pl.multiple_of(j * tk, tk)
        k = k_ref[pl.ds(start, tk), :]
        q = q_ref[...]
        zero = jnp.zeros_like(q)
        z_ref[0] = _dot_nt(jnp.where(low, q, zero), k)
        z_ref[1] = _dot_nt(jnp.where(low, zero, q), k)

    def absorb(j, z_ref, masked, width=tk):
        start = pl.multiple_of(j * tk, tk)
        v = v_ref[pl.ds(start, width), :]
        tri = tri_ref[...]
        if masked:
            diff = lax.broadcasted_iota(jnp.int32, (tq, width), 1) - lax.broadcasted_iota(jnp.int32, (tq, width), 0)
            keep = diff < i * tq - start
        staged = []
        for hh in range(2):
            z = z_ref[hh, :, :width]
            neg_abs = lax.bitcast_convert_type(lax.bitcast_convert_type(z, jnp.uint32) | sign, F32)
            sp = jnp.maximum(z, 0.0) + jnp.log2(1.0 + jnp.exp2(neg_abs))
            if masked:
                sp = jnp.where(keep, sp, 0.0)
            c = c_ref[hh]
            parts = []
            for cc in reversed(range(width // tc)):
                cs = slice(cc * tc, (cc + 1) * tc)
                spc = sp[:, cs]
                right = _dot(spc.astype(BF16), tri)
                parts.append((cs, z[:, cs] - spc, right, c))
                c = c + right[:, 0:1] + spc[:, 0:1]
            c_ref[hh] = c
            staged.append(parts)
        for hh in range(2):
            acc = acc_ref[hh]
            for cs, log_beta, right, c in staged[hh]:
                a = jnp.exp2(log_beta - right - c)
                if masked:
                    a = jnp.where(keep[:, cs], a, 0.0)
                acc = acc + _dot(a.astype(BF16), v[cs, :])
            acc_ref[hh] = acc

    def live():
        return jnp.min(c_ref[...]) < SB_DEAD_BITS

    def pairs(first_ref, second_ref):
        npair = n_full // 2

        def body(t, carry):
            top = 2 * (npair - t) - 1

            @pl.when(live())
            def _():
                logits(top - 1, second_ref)
                absorb(top, first_ref, False)

            @pl.when(live())
            def _():
                logits(jnp.maximum(top - 2, 0), first_ref)
                absorb(top - 1, second_ref, False)

            return carry

        lax.fori_loop(0, npair, body, 0)

    def absorb_diagonal(z_ref, next_ref):
        blocks = (i * tq - n_full * tk) // tq + 1
        for nb in range(1, tk // tq + 1):
            @pl.when(blocks == nb)
            def _():
                logits(jnp.maximum(n_full - 1, 0), next_ref)
                absorb(n_full, z_ref, True, nb * tq)

    logits(n_full, za_ref)
    absorb_diagonal(za_ref, zb_ref)

    @pl.when(n_full % 2 == 1)
    def _():
        @pl.when(live())
        def _():
            logits(jnp.maximum(n_full - 2, 0), za_ref)
            absorb(n_full - 1, zb_ref, False)

        pairs(za_ref, zb_ref)

    @pl.when(n_full % 2 == 0)
    def _():
        pairs(zb_ref, za_ref)

    o_ref[...] = jnp.where(low, acc_ref[0], acc_ref[1])


def _sb_attention(q, k, v, tri, batch, seq):
    tq, tk, tc = TQ_ATTN, TK_ATTN, TC_SB
    q = q.reshape(batch, seq, -1)
    k = k.reshape(batch, seq, -1)
    v = v.reshape(batch, seq, -1)
    out = pl.pallas_call(
        functools.partial(_sb_kernel, tq=tq, tk=tk, tc=tc),
        grid=(batch, SB_HEADS // 2, seq // tq),
        in_specs=[pl.BlockSpec((None, tq, LANES), lambda b, p, i: (b, i, p)),
                  pl.BlockSpec((None, seq, LANES), lambda b, p, i: (b, 0, p)),
                  pl.BlockSpec((None, seq, LANES), lambda b, p, i: (b, 0, p)),
                  pl.BlockSpec((tc, tc), lambda b, p, i: (0, 0))],
        out_specs=pl.BlockSpec((None, tq, LANES), lambda b, p, i: (b, i, p)),
        out_shape=jax.ShapeDtypeStruct((batch, seq, WIDTH_C), F32),
        scratch_shapes=[pltpu.VMEM((2, tq, tk), F32), pltpu.VMEM((2, tq, tk), F32),
                        pltpu.VMEM((2, tq, 1), F32), pltpu.VMEM((2, tq, LANES), F32)],
        compiler_params=pltpu.CompilerParams(dimension_semantics=("parallel", "parallel", "arbitrary"),
                                             vmem_limit_bytes=VMEM_LIMIT),
        name="sb_attention",
    )(q, k, v, tri)
    return out.reshape(batch * seq, WIDTH_C)


def _swa_kernel(slopes_ref, sinks_ref, q_ref, k_ref, v_ref, o_ref, *, tq):
    w = SWA_WINDOW
    group = SWA_Q_HEADS // SWA_KV_HEADS
    hkv = pl.program_id(1)
    i = pl.program_id(2)
    lane = lax.broadcasted_iota(jnp.int32, (1, LANES), 1)
    low = lane < HEAD_DIM
    row = lax.broadcasted_iota(jnp.int32, (w, 2 * w), 0)
    col = lax.broadcasted_iota(jnp.int32, (w, 2 * w), 1)
    for n in range(tq // w):
        q0 = i * tq + n * w
        ks = pl.multiple_of(jnp.maximum(q0 - w, 0), w)
        k = k_ref[pl.ds(ks, 2 * w), :]
        v = v_ref[pl.ds(ks, 2 * w), :]
        dist = (q0 - ks) + row - col
        valid = (dist >= 0) & (dist < w)
        distf = dist.astype(F32)
        for pr in range(group // 2):
            qp = q_ref[n * w:(n + 1) * w, LANES * pr:LANES * (pr + 1)]
            zero = jnp.zeros_like(qp)
            outs = []
            for half in range(2):
                head = hkv * group + pr * 2 + half
                qh = jnp.where(low, qp, zero) if half == 0 else jnp.where(low, zero, qp)
                s = _dot_nt(qh, k) - slopes_ref[head] * distf
                s = jnp.where(valid, s, NEG)
                sink = sinks_ref[head]
                m = jnp.maximum(jnp.max(s, axis=-1, keepdims=True), sink)
                p = jnp.exp(s - m)
                den = jnp.sum(p, axis=-1, keepdims=True) + jnp.exp(sink - m)
                outs.append(_dot(p.astype(BF16), v) / den)
            o_ref[n * w:(n + 1) * w, LANES * pr:LANES * (pr + 1)] = jnp.where(low, outs[0], outs[1])


def _swa_attention(q, k, v, slopes, sinks, batch, seq):
    tq = TQ_SWA
    q = q.reshape(batch, seq, -1)
    k = k.reshape(batch, seq, -1)
    v = v.reshape(batch, seq, -1)
    smem = pl.BlockSpec(memory_space=pltpu.SMEM)
    out = pl.pallas_call(
        functools.partial(_swa_kernel, tq=tq),
        grid=(batch, SWA_KV_HEADS, seq // tq),
        in_specs=[smem, smem,
                  pl.BlockSpec((None, tq, 2 * LANES), lambda b, h, i: (b, i, h)),
                  pl.BlockSpec((None, seq, LANES), lambda b, h, i: (b, 0, h)),
                  pl.BlockSpec((None, seq, LANES), lambda b, h, i: (b, 0, h))],
        out_specs=pl.BlockSpec((None, tq, 2 * LANES), lambda b, h, i: (b, i, h)),
        out_shape=jax.ShapeDtypeStruct((batch, seq, WIDTH_B), F32),
        compiler_params=pltpu.CompilerParams(dimension_semantics=("parallel", "parallel", "arbitrary"),
                                             vmem_limit_bytes=VMEM_LIMIT),
        name="swa_attention",
    )(slopes, sinks, q, k, v)
    return out.reshape(batch * seq, WIDTH_B)


def _out_proj_kernel(*refs, moe):
    if moe:
        oa, ob, oc, x_ref, gout, wout, gffn, wr_hi, wr_lo, xo_ref, h_ref, sel_ref = refs
    else:
        oa, ob, oc, x_ref, gout, wout, gffn, xo_ref, h_ref = refs
    g = gout[...]
    b0, b1 = WIDTH_A, WIDTH_A + WIDTH_B
    na = _rms(oa[...], g[:, :b0]).astype(BF16)
    nb = _rms(ob[...], g[:, b0:b1]).astype(BF16)
    nc = _rms(oc[...], g[:, b1:]).astype(BF16)
    xn = x_ref[...] + _dot(na, wout[:b0, :]) + _dot(nb, wout[b0:b1, :]) + _dot(nc, wout[b1:, :])
    xo_ref[...] = xn
    hf = _rms(xn, gffn[...])
    hb = hf.astype(BF16)
    h_ref[...] = hf if moe else hb
    if moe:
        lo = (hf - hb.astype(F32)).astype(BF16)
        logits = _dot(hb, wr_hi[...]) + _dot(lo, wr_hi[...]) + _dot(hb, wr_lo[...])
        lane = lax.broadcasted_iota(jnp.int32, logits.shape, 1)
        lg = jnp.where(lane < N_EXPERTS, logits, -jnp.inf)
        m1 = jnp.max(lg, axis=-1, keepdims=True)
        i1 = jnp.min(jnp.where(lg == m1, lane, LANES), axis=-1, keepdims=True)
        lg2 = jnp.where(lane == i1, -jnp.inf, lg)
        m2 = jnp.max(lg2, axis=-1, keepdims=True)
        i2 = jnp.min(jnp.where(lg2 == m2, lane, LANES), axis=-1, keepdims=True)
        e = jnp.exp(m2 - m1)
        g1 = 1.0 / (1.0 + e)
        g2 = e / (1.0 + e)
        sel_ref[...] = jnp.where(lane == 0, i1.astype(F32),
                                 jnp.where(lane == 1, i2.astype(F32),
                                           jnp.where(lane == 2, g1, jnp.where(lane == 3, g2, 0.0))))


def _out_proj(oa, ob, oc, x, lw, moe):
    n = x.shape[0]
    tm = TM_PROJ
    row = lambda i: (i, 0)
    fixed = lambda i: (0, 0)
    full = lambda a: pl.BlockSpec(a.shape, fixed)
    args = [oa, ob, oc, x, lw["out_g"], lw["w_out"], lw["ffn_g"]]
    in_specs = [pl.BlockSpec((tm, WIDTH_A), row), pl.BlockSpec((tm, WIDTH_B), row), pl.BlockSpec((tm, WIDTH_C), row),
                pl.BlockSpec((tm, D_MODEL), row), full(lw["out_g"]), full(lw["w_out"]), full(lw["ffn_g"])]
    out_specs = [pl.BlockSpec((tm, D_MODEL), row), pl.BlockSpec((tm, D_MODEL), row)]
    out_shape = [jax.ShapeDtypeStruct((n, D_MODEL), F32), jax.ShapeDtypeStruct((n, D_MODEL), F32 if moe else BF16)]
    if moe:
        args += [lw["router_hi"], lw["router_lo"]]
        in_specs += [full(lw["router_hi"]), full(lw["router_lo"])]
        out_specs.append(pl.BlockSpec((tm, LANES), row))
        out_shape.append(jax.ShapeDtypeStruct((n, LANES), F32))
    return pl.pallas_call(
        functools.partial(_out_proj_kernel, moe=moe),
        grid=(n // tm,),
        in_specs=in_specs, out_specs=out_specs, out_shape=out_shape,
        compiler_params=pltpu.CompilerParams(dimension_semantics=("parallel",), vmem_limit_bytes=VMEM_LIMIT),
        name="out_proj_moe" if moe else "out_proj",
    )(*args)


def _swiglu_partial(h, wg, wu, wd):
    h = h.astype(BF16)
    g = _dot(h, wg)
    u = _dot(h, wu)
    return _dot((g * jax.nn.sigmoid(g) * u).astype(BF16), wd)


def _ffn_kernel(h_ref, wg_ref, wu_ref, wd_ref, x_ref, o_ref):
    j = pl.program_id(1)
    y = _swiglu_partial(h_ref[...], wg_ref[...], wu_ref[...], wd_ref[...])

    @pl.when(j == 0)
    def _():
        o_ref[...] = x_ref[...] + y

    @pl.when(j > 0)
    def _():
        o_ref[...] += y


def _dense_ffn(h, x, lw):
    n = x.shape[0]
    tm, tf = TM_FFN, TF_FFN
    return pl.pallas_call(
        _ffn_kernel,
        grid=(n // tm, D_FF // tf),
        in_specs=[pl.BlockSpec((tm, D_MODEL), lambda i, j: (i, 0)),
                  pl.BlockSpec((D_MODEL, tf), lambda i, j: (0, j)),
                  pl.BlockSpec((D_MODEL, tf), lambda i, j: (0, j)),
                  pl.BlockSpec((tf, D_MODEL), lambda i, j: (j, 0)),
                  pl.BlockSpec((tm, D_MODEL), lambda i, j: (i, 0))],
        out_specs=pl.BlockSpec((tm, D_MODEL), lambda i, j: (i, 0)),
        out_shape=jax.ShapeDtypeStruct((n, D_MODEL), F32),
        compiler_params=pltpu.CompilerParams(dimension_semantics=("parallel", "arbitrary"),
                                             vmem_limit_bytes=VMEM_LIMIT),
        name="dense_ffn",
    )(h, lw["w_gate"], lw["w_up"], lw["w_down"], x)


def _moe_kernel(blk_e_ref, n_used_ref, xs_ref, wg_ref, wu_ref, wd_ref, o_ref):
    i = pl.program_id(0)
    j = pl.program_id(1)
    used = i < n_used_ref[0]

    @pl.when(used)
    def _():
        y = _swiglu_partial(xs_ref[...], wg_ref[...], wu_ref[...], wd_ref[...])

        @pl.when(j == 0)
        def _():
            o_ref[...] = y

        @pl.when(j > 0)
        def _():
            o_ref[...] += y

    @pl.when(jnp.logical_not(used) & (j == 0))
    def _():
        o_ref[...] = jnp.zeros_like(o_ref)


def _moe_experts(xs, blk_e, n_used, lw):
    p = xs.shape[0]
    tm, tf = TM_MOE, TF_MOE
    n_ff = D_FF_EXPERT // tf

    def col(i, j):
        return jnp.where(i % 2 == 0, j, n_ff - 1 - j)

    grid_spec = pltpu.PrefetchScalarGridSpec(
        num_scalar_prefetch=2,
        grid=(p // tm, n_ff),
        in_specs=[pl.BlockSpec((tm, D_MODEL), lambda i, j, be, nu: (i, 0)),
                  pl.BlockSpec((None, D_MODEL, tf), lambda i, j, be, nu: (be[i], 0, col(i, j))),
                  pl.BlockSpec((None, D_MODEL, tf), lambda i, j, be, nu: (be[i], 0, col(i, j))),
                  pl.BlockSpec((None, tf, D_MODEL), lambda i, j, be, nu: (be[i], col(i, j), 0))],
        out_specs=pl.BlockSpec((tm, D_MODEL), lambda i, j, be, nu: (i, 0)),
    )
    return pl.pallas_call(
        _moe_kernel,
        grid_spec=grid_spec,
        out_shape=jax.ShapeDtypeStruct((p, D_MODEL), F32),
        compiler_params=pltpu.CompilerParams(dimension_semantics=("arbitrary", "arbitrary"),
                                             vmem_limit_bytes=VMEM_LIMIT),
        name="moe_experts",
    )(blk_e, n_used, xs, lw["w_gate"], lw["w_up"], lw["w_down"])


def _row_copy(src_ref, src_row, dst_ref, dst_row, sem):
    return pltpu.make_async_copy(src_ref.at[pl.ds(src_row, 1)], dst_ref.at[pl.ds(dst_row, 1)], sem)


def _dispatch_kernel(dest_ref, h_ref, xs_in_ref, xs_ref, sem, *, tm):
    del xs_in_ref

    def start(r, carry):
        for k in range(2):
            _row_copy(h_ref, r, xs_ref, dest_ref[0, 2 * r + k], sem).start()
        return carry

    def wait(r, carry):
        for k in range(2):
            _row_copy(h_ref, r, xs_ref, dest_ref[0, 2 * r + k], sem).wait()
        return carry

    lax.fori_loop(0, tm, start, 0, unroll=8)
    lax.fori_loop(0, tm, wait, 0, unroll=8)


def _dispatch(dest_blk, h, p):
    n = h.shape[0]
    tm = TM_ROUTE
    xs0 = jnp.zeros((p, D_MODEL), F32)
    return pl.pallas_call(
        functools.partial(_dispatch_kernel, tm=tm),
        grid=(n // tm,),
        in_specs=[pl.BlockSpec((None, 1, 2 * tm), lambda i: (i, 0, 0), memory_space=pltpu.SMEM),
                  pl.BlockSpec((tm, D_MODEL), lambda i: (i, 0)),
                  pl.BlockSpec(memory_space=pl.ANY)],
        out_specs=pl.BlockSpec(memory_space=pl.ANY),
        out_shape=jax.ShapeDtypeStruct((p, D_MODEL), F32),
        scratch_shapes=[pltpu.SemaphoreType.DMA(())],
        input_output_aliases={2: 0},
        compiler_params=pltpu.CompilerParams(dimension_semantics=("arbitrary",), vmem_limit_bytes=VMEM_LIMIT,
                                             has_side_effects=True),
        name="moe_dispatch",
    )(dest_blk, h, xs0)


def _combine_kernel(dest_ref, ys_ref, x_ref, sel_ref, o_ref, ybuf, sem, *, tm):
    def start(r, carry):
        for k in range(2):
            _row_copy(ys_ref, dest_ref[0, 2 * r + k], ybuf.at[k], r, sem).start()
        return carry

    def wait(r, carry):
        for k in range(2):
            _row_copy(ys_ref, dest_ref[0, 2 * r + k], ybuf.at[k], r, sem).wait()
        return carry

    lax.fori_loop(0, tm, start, 0, unroll=8)
    lax.fori_loop(0, tm, wait, 0, unroll=8)
    sel = sel_ref[...]
    o_ref[...] = x_ref[...] + ybuf[0] * sel[:, 2:3] + ybuf[1] * sel[:, 3:4]


def _combine(dest_blk, ys, x, sel):
    n = x.shape[0]
    tm = TM_ROUTE
    return pl.pallas_call(
        functools.partial(_combine_kernel, tm=tm),
        grid=(n // tm,),
        in_specs=[pl.BlockSpec((None, 1, 2 * tm), lambda i: (i, 0, 0), memory_space=pltpu.SMEM),
                  pl.BlockSpec(memory_space=pl.ANY),
                  pl.BlockSpec((tm, D_MODEL), lambda i: (i, 0)),
                  pl.BlockSpec((tm, LANES), lambda i: (i, 0))],
        out_specs=pl.BlockSpec((tm, D_MODEL), lambda i: (i, 0)),
        out_shape=jax.ShapeDtypeStruct((n, D_MODEL), F32),
        scratch_shapes=[pltpu.VMEM((2, tm, D_MODEL), F32), pltpu.SemaphoreType.DMA(())],
        compiler_params=pltpu.CompilerParams(dimension_semantics=("arbitrary",), vmem_limit_bytes=VMEM_LIMIT),
        name="moe_combine",
    )(dest_blk, ys, x, sel)


def _moe_ffn(h, x, sel, lw):
    n = x.shape[0]
    tm = TM_MOE
    top_idx = sel[:, :2].astype(jnp.int32)
    flat_e = top_idx.reshape(-1)
    onehot = (flat_e[:, None] == jnp.arange(N_EXPERTS, dtype=jnp.int32)[None, :]).astype(jnp.int32)
    csum = jnp.cumsum(onehot, axis=0)
    rank = jnp.sum(csum * onehot, axis=1) - 1
    counts = csum[-1]
    padded = (counts + tm - 1) // tm * tm
    pend = jnp.cumsum(padded)
    pstart = pend - padded
    dest = pstart[flat_e] + rank
    p = n * 2 + N_EXPERTS * tm
    n_blk = p // tm
    blk_start = jnp.arange(n_blk, dtype=jnp.int32) * tm
    blk_e = jnp.minimum(jnp.sum((pend[None, :] <= blk_start[:, None]).astype(jnp.int32), axis=1), N_EXPERTS - 1)
    n_used = (pend[-1:] // tm).astype(jnp.int32)
    dest_blk = dest.astype(jnp.int32).reshape(n // TM_ROUTE, 1, 2 * TM_ROUTE)
    xs = _dispatch(dest_blk, h, p)
    ys = _moe_experts(xs, blk_e, n_used, lw)
    return _combine(dest_blk, ys, x, sel)


def _pad_cols(w, width):
    return jnp.pad(w, ((0, 0), (0, width - w.shape[1])))


def _head_groups(w, heads, dim):
    kdim = w.shape[0]
    return jnp.pad(w.reshape(kdim, heads, dim), ((0, 0), (0, 0), (0, LANES - dim))).reshape(kdim, heads * LANES)


def _layer_weights(layer, p):
    w_in = p["w_in"][layer]
    a, b, c = w_in[:, :COLS_A], w_in[:, COLS_A:COLS_A + COLS_B], w_in[:, COLS_A + COLS_B:]
    kpe = jnp.pad(a[:, MLA_Q_RANK + MLA_KV_RANK:], ((0, 0), (MLA_NOPE, LANES - MLA_QK)))
    bq = b[:, :WIDTH_B]

    def dup(w):
        return jnp.repeat(w.reshape(D_MODEL, SWA_KV_HEADS, 1, HEAD_DIM), 2, axis=2).reshape(D_MODEL, 2 * LANES)

    bk = dup(b[:, WIDTH_B:WIDTH_B + SWA_KV_HEADS * HEAD_DIM])
    bv = dup(b[:, WIDTH_B + SWA_KV_HEADS * HEAD_DIM:])
    w_in_p = jnp.concatenate([a[:, :MLA_Q_RANK + MLA_KV_RANK], kpe, bq, bk, bv, c], axis=1).astype(BF16)
    w_ukv = p["mla_w_ukv"][layer].reshape(MLA_KV_RANK, MLA_HEADS, MLA_NOPE + MLA_V)
    w_uk = _head_groups(w_ukv[:, :, :MLA_NOPE].reshape(MLA_KV_RANK, -1), MLA_HEADS, MLA_NOPE)
    w_uv = w_ukv[:, :, MLA_NOPE:].reshape(MLA_KV_RANK, -1)
    row = lambda g: g.reshape(1, -1).astype(F32)
    lw = {
        "attn_g": row(p["attn_norm_g"][layer]),
        "w_in": w_in_p,
        "cq_g": row(p["mla_cq_norm_g"][layer]),
        "w_uq": _head_groups(p["mla_w_uq"][layer], MLA_HEADS, MLA_QK).astype(BF16),
        "ckv_g": row(p["mla_ckv_norm_g"][layer]),
        "w_ukv": jnp.concatenate([w_uk, w_uv], axis=1).astype(BF16),
        "mla_qg": _pad_cols(row(p["mla_qn_g"][layer]), LANES),
        "mla_kg": _pad_cols(row(p["mla_kn_g"][layer]), LANES),
        "swa_qg": jnp.tile(row(p["swa_qn_g"][layer]), (1, 2)),
        "swa_kg": jnp.tile(row(p["swa_kn_g"][layer]), (1, 2)),
        "sinks": p["swa_sinks"][layer].astype(F32),
        "out_g": row(p["out_norm_g"][layer]),
        "w_out": p["w_out"][layer].astype(BF16),
        "ffn_g": row(p["ffn_norm_g"][layer]),
    }
    j = layer // 2
    if layer % 2 == 0:
        lw["w_gate"] = p["dense_w_gate"][j].astype(BF16)
        lw["w_up"] = p["dense_w_up"][j].astype(BF16)
        lw["w_down"] = p["dense_w_down"][j].astype(BF16)
    else:
        rw = _pad_cols(p["router_w"][j], LANES)
        hi = rw.astype(BF16)
        lw["router_hi"] = hi
        lw["router_lo"] = (rw - hi.astype(F32)).astype(BF16)
        lw["w_gate"] = p["moe_w_gate"][j].astype(BF16)
        lw["w_up"] = p["moe_w_up"][j].astype(BF16)
        lw["w_down"] = p["moe_w_down"][j].astype(BF16)
    return lw


def _rope_tables(seq):
    half = MLA_ROPE // 2
    inv = ROPE_THETA ** (-jnp.arange(half, dtype=F32) / half)
    ang = jnp.arange(seq).astype(F32)[:, None] * inv[None, :]
    cos, sin = jnp.cos(ang), jnp.sin(ang)
    zl = jnp.zeros((seq, MLA_NOPE), F32)
    zh = jnp.zeros((seq, half), F32)
    zr = jnp.zeros((seq, LANES - MLA_QK), F32)
    cos_t = jnp.concatenate([zl + 1.0, cos, cos, zr + 1.0], axis=1)
    sina = jnp.concatenate([zl, -sin, zh, zr], axis=1)
    sinb = jnp.concatenate([zl, zh, sin, zr], axis=1)
    return cos_t, sina, sinb


def kernel(x, attn_norm_g, w_in, mla_cq_norm_g, mla_w_uq, mla_ckv_norm_g, mla_w_ukv, mla_qn_g, mla_kn_g, swa_qn_g, swa_kn_g, swa_sinks, out_norm_g, w_out, ffn_norm_g, dense_w_gate, dense_w_up, dense_w_down, router_w, moe_w_gate, moe_w_up, moe_w_down):
    params = dict(attn_norm_g=attn_norm_g, w_in=w_in, mla_cq_norm_g=mla_cq_norm_g, mla_w_uq=mla_w_uq,
                  mla_ckv_norm_g=mla_ckv_norm_g, mla_w_ukv=mla_w_ukv, mla_qn_g=mla_qn_g, mla_kn_g=mla_kn_g,
                  swa_qn_g=swa_qn_g, swa_kn_g=swa_kn_g, swa_sinks=swa_sinks, out_norm_g=out_norm_g, w_out=w_out,
                  ffn_norm_g=ffn_norm_g, dense_w_gate=dense_w_gate, dense_w_up=dense_w_up,
                  dense_w_down=dense_w_down, router_w=router_w, moe_w_gate=moe_w_gate, moe_w_up=moe_w_up,
                  moe_w_down=moe_w_down)
    batch, seq, _ = x.shape
    depth = w_in.shape[0]
    tabs = _rope_tables(seq)
    slopes = jnp.exp2(-8.0 * jnp.arange(1, SWA_Q_HEADS + 1, dtype=F32) / SWA_Q_HEADS)
    tri = (jnp.arange(TC_SB)[:, None] > jnp.arange(TC_SB)[None, :]).astype(BF16)
    xf =x.reshape(batch * seq, D_MODEL)
    for layer in range(depth):
        lw = _layer_weights(layer, params)
        mq, mk, mv, sq, sk, sv, cq, ck, cv = _in_proj(xf, lw, tabs, seq)
        o_a = _mla_attention(mq, mk, mv, batch, seq)
        o_b = _swa_attention(sq, sk, sv, slopes, lw["sinks"], batch, seq)
        o_c = _sb_attention(cq, ck, cv, tri, batch, seq)
        if layer % 2 == 0:
            xf, h = _out_proj(o_a, o_b, o_c, xf, lw, moe=False)
            xf = _dense_ffn(h, xf, lw)
        else:
            xf, h, sel = _out_proj(o_a, o_b, o_c, xf, lw, moe=True)
            xf = _moe_ffn(h, xf, sel, lw)
    return xf.reshape(batch, seq, D_MODEL)
```

```python
import functools

import jax
import jax.numpy as jnp
from jax import lax
from jax.experimental import pallas as pl
from jax.experimental.pallas import tpu as pltpu

F32 = jnp.float32
BF16 = jnp.bfloat16

D_MODEL = 1024
HEAD_DIM = 64
MLA_HEADS = 4
MLA_Q_RANK = 256
MLA_KV_RANK = 128
MLA_NOPE = 64
MLA_ROPE = 32
MLA_QK = MLA_NOPE + MLA_ROPE
MLA_V = 64
ROPE_THETA = 10000.0
SWA_Q_HEADS = 8
SWA_KV_HEADS = 2
SWA_WINDOW = 128
SB_HEADS = 4
WIDTH_A = MLA_HEADS * MLA_V
WIDTH_B = SWA_Q_HEADS * HEAD_DIM
WIDTH_C = SB_HEADS * HEAD_DIM
COLS_A = MLA_Q_RANK + MLA_KV_RANK + MLA_ROPE
COLS_B = (SWA_Q_HEADS + 2 * SWA_KV_HEADS) * HEAD_DIM
D_FF = 2816
N_EXPERTS = 8
D_FF_EXPERT = 3584
RMS_EPS = 1e-6
NEG = -1e30
LOG2E = 1.4426950408889634

LANES = 128
VMEM_LIMIT = 56 * 1024 * 1024

TM_PROJ = 512
TM_IN = 1024
TQ_ATTN = 256
TK_ATTN = 1024
TK_SB = 256
TC_SB = 256
SB_DEAD_BITS = 151.0
TQ_SWA = 512
TM_FFN = 512
TF_FFN = 1408
TM_MOE = 512
TM_ROUTE = 512
TF_MOE = 1792

P_CQ = 0
P_CKV = 256
P_KPE = 384
P_SWQ = 512
P_SWK = 1024
P_SWV = 1280
P_SBQ = 1536
P_SBK = 1792
P_SBV = 2048
P_COLS = 2304


def _dot(a, b):
    return jnp.dot(a, b, preferred_element_type=F32)


def _dot_nt(a, b):
    return lax.dot_general(a, b, (((1,), (1,)), ((), ())), preferred_element_type=F32)


def _rms(x, g):
    return x * lax.rsqrt(jnp.mean(x * x, axis=-1, keepdims=True) + RMS_EPS) * g


def _in_proj_kernel(x_ref, g_ref, w_ref, gcq_ref, wuq_ref, gckv_ref, wukv_ref, gq_ref, gk_ref,
                    cos_ref, sina_ref, sinb_ref, sgq_ref, sgk_ref,
                    mq_ref, mk_ref, mv_ref, sq_ref, sk_ref, sv_ref, cq_ref, ck_ref, cv_ref, *, sub):
    lane = lax.broadcasted_iota(jnp.int32, (1, LANES), 1)
    low = lane < HEAD_DIM
    scale = HEAD_DIM ** -0.5

    def head_norm(t, g):
        return t * lax.rsqrt(jnp.sum(t * t, axis=-1, keepdims=True) * (1.0 / MLA_QK) + RMS_EPS) * g

    def pair_norm(t, g):
        t2 = t * t
        lo = jnp.sum(jnp.where(low, t2, 0.0), axis=-1, keepdims=True)
        hi = jnp.sum(jnp.where(low, 0.0, t2), axis=-1, keepdims=True)
        ms = jnp.where(low, lo, hi) * (1.0 / HEAD_DIM)
        return t * lax.rsqrt(ms + RMS_EPS) * g

    def rows(rs):
        h = _rms(x_ref[rs, :], g_ref[...]).astype(BF16)

        def proj(lo, hi):
            return _dot(h, w_ref[:, lo:hi])

        cos = cos_ref[rs, :]
        sina = sina_ref[rs, :]
        sinb = sinb_ref[rs, :]

        def rope(t):
            return t * cos + pltpu.roll(t, LANES - MLA_ROPE // 2, 1) * sina + pltpu.roll(t, MLA_ROPE // 2, 1) * sinb

        a = proj(P_CQ, P_SWQ)
        q = _dot(_rms(a[:, :MLA_Q_RANK], gcq_ref[...]).astype(BF16), wuq_ref[...])
        for hh in range(MLA_HEADS):
            sl = slice(LANES * hh, LANES * (hh + 1))
            mq_ref[rs, sl] = (rope(head_norm(q[:, sl], gq_ref[...])) * (MLA_QK ** -0.5 * LOG2E)).astype(BF16)
        kv = _dot(_rms(a[:, P_CKV:P_KPE], gckv_ref[...]).astype(BF16), wukv_ref[...])
        kpe = a[:, P_KPE:P_SWQ]
        for hh in range(MLA_HEADS):
            sl = slice(LANES * hh, LANES * (hh + 1))
            mk_ref[rs, sl] = rope(head_norm(kv[:, sl] + kpe, gk_ref[...])).astype(BF16)
        mv_ref[rs, :] = kv[:, MLA_HEADS * LANES:].astype(BF16)
        bq = proj(P_SWQ, P_SWK)
        for gi in range(WIDTH_B // LANES):
            sl = slice(LANES * gi, LANES * (gi + 1))
            sq_ref[rs, sl] = (pair_norm(bq[:, sl], sgq_ref[...]) * scale).astype(BF16)
        bk = proj(P_SWK, P_SWV)
        for gi in range(SWA_KV_HEADS):
            sl = slice(LANES * gi, LANES * (gi + 1))
            sk_ref[rs, sl] = pair_norm(bk[:, sl], sgk_ref[...]).astype(BF16)
        sv_ref[rs, :] = proj(P_SWV, P_SBQ).astype(BF16)
        cq_ref[rs, :] = (proj(P_SBQ, P_SBK) * (scale * LOG2E)).astype(BF16)
        ck_ref[rs, :] = proj(P_SBK, P_SBV).astype(BF16)
        cv_ref[rs, :] = proj(P_SBV, P_COLS).astype(BF16)

    for r0 in range(0, x_ref.shape[0], sub):
        rows(slice(r0, r0 + sub))


def _in_proj(x, lw, tabs, seq):
    n = x.shape[0]
    tm = TM_IN
    nt = seq // tm
    row = lambda i: (i, 0)
    fixed = lambda i: (0, 0)
    pos = lambda i: (i % nt, 0)
    full = lambda a: pl.BlockSpec(a.shape, fixed)
    cos, sina, sinb = tabs
    widths = (4 * LANES, 4 * LANES, WIDTH_A, WIDTH_B, 2 * LANES, 2 * LANES, WIDTH_C, WIDTH_C, WIDTH_C)
    return pl.pallas_call(
        functools.partial(_in_proj_kernel, sub=TM_PROJ),
        grid=(n // tm,),
        in_specs=[pl.BlockSpec((tm, D_MODEL), row), full(lw["attn_g"]), full(lw["w_in"]), full(lw["cq_g"]),
                  full(lw["w_uq"]), full(lw["ckv_g"]), full(lw["w_ukv"]), full(lw["mla_qg"]), full(lw["mla_kg"]),
                  pl.BlockSpec((tm, LANES), pos), pl.BlockSpec((tm, LANES), pos), pl.BlockSpec((tm, LANES), pos),
                  full(lw["swa_qg"]), full(lw["swa_kg"])],
        out_specs=[pl.BlockSpec((tm, w), row) for w in widths],
        out_shape=[jax.ShapeDtypeStruct((n, w), BF16) for w in widths],
        compiler_params=pltpu.CompilerParams(dimension_semantics=("parallel",), vmem_limit_bytes=VMEM_LIMIT),
        name="in_proj",
    )(x, lw["attn_g"], lw["w_in"], lw["cq_g"], lw["w_uq"], lw["ckv_g"], lw["w_ukv"], lw["mla_qg"], lw["mla_kg"],
      cos, sina, sinb, lw["swa_qg"], lw["swa_kg"])


def _mla_kernel(q_ref, k_ref, v_ref, o_ref, sa_ref, sb_ref, m_ref, l_ref, acc_ref, *, tq, tk):
    i = pl.program_id(2)
    n_full = (i * tq) // tk
    m_ref[...] = jnp.full(m_ref.shape, NEG, F32)
    l_ref[...] = jnp.zeros(l_ref.shape, F32)
    acc_ref[...] = jnp.zeros(acc_ref.shape, F32)

    def scores(j, s_ref):
        start = pl.multiple_of(j * tk, tk)
        k = k_ref[pl.ds(start, tk), :]
        for hh in range(2):
            sl = slice(LANES * hh, LANES * (hh + 1))
            s_ref[hh] = _dot_nt(q_ref[:, sl], k[:, sl])

    def absorb(j, s_ref, masked, width=tk):
        start = pl.multiple_of(j * tk, tk)
        v = v_ref[pl.ds(start, width), :]
        if masked:
            diff = lax.broadcasted_iota(jnp.int32, (tq, width), 1) - lax.broadcasted_iota(jnp.int32, (tq, width), 0)
            keep = diff <= i * tq - start
        for hh in range(2):
            s = s_ref[hh, :, :width]
            if masked:
                s = jnp.where(keep, s, NEG)
            m = m_ref[hh]
            m_new = jnp.maximum(m, jnp.max(s, axis=-1, keepdims=True))
            alpha = jnp.exp2(m - m_new)
            p = jnp.exp2(s - m_new)
            l_ref[hh] = alpha * l_ref[hh] + jnp.sum(p, axis=-1, keepdims=True)
            acc_ref[hh] = alpha * acc_ref[hh] + _dot(p.astype(BF16), v)
            m_ref[hh] = m_new

    scores(0, sa_ref)

    def pair(t, carry):
        scores(2 * t + 1, sb_ref)
        absorb(2 * t, sa_ref, False)
        scores(2 * t + 2, sa_ref)
        absorb(2 * t + 1, sb_ref, False)
        return carry

    lax.fori_loop(0, n_full // 2, pair, 0)
    base = 2 * (n_full // 2)

    def absorb_diagonal(s_ref):
        blocks = (i * tq - n_full * tk) // tq + 1
        for nb in range(1, tk // tq + 1):
            pl.when(blocks == nb)(functools.partial(absorb, n_full, s_ref, True, nb * tq))

    @pl.when(n_full % 2 == 1)
    def _():
        scores(base + 1, sb_ref)
        absorb(base, sa_ref, False)
        absorb_diagonal(sb_ref)

    @pl.when(n_full % 2 == 0)
    def _():
        absorb_diagonal(sa_ref)

    lane = lax.broadcasted_iota(jnp.int32, (1, LANES), 1)
    o_ref[...] = jnp.where(lane < MLA_V, acc_ref[0] / l_ref[0], acc_ref[1] / l_ref[1])


def _mla_attention(q, k, v, batch, seq):
    tq, tk = TQ_ATTN, TK_ATTN
    q = q.reshape(batch, seq, -1)
    k = k.reshape(batch, seq, -1)
    v = v.reshape(batch, seq, -1)
    out = pl.pallas_call(
        functools.partial(_mla_kernel, tq=tq, tk=tk),
        grid=(batch, MLA_HEADS // 2, seq // tq),
        in_specs=[pl.BlockSpec((None, tq, 2 * LANES), lambda b, p, i: (b, i, p)),
                  pl.BlockSpec((None, seq, 2 * LANES), lambda b, p, i: (b, 0, p)),
                  pl.BlockSpec((None, seq, LANES), lambda b, p, i: (b, 0, p))],
        out_specs=pl.BlockSpec((None, tq, LANES), lambda b, p, i: (b, i, p)),
        out_shape=jax.ShapeDtypeStruct((batch, seq, WIDTH_A), F32),
        scratch_shapes=[pltpu.VMEM((2, tq, tk), F32), pltpu.VMEM((2, tq, tk), F32),
                        pltpu.VMEM((2, tq, 1), F32), pltpu.VMEM((2, tq, 1), F32), pltpu.VMEM((2, tq, LANES), F32)],
        compiler_params=pltpu.CompilerParams(dimension_semantics=("parallel", "parallel", "arbitrary"),
                                             vmem_limit_bytes=VMEM_LIMIT),
        name="mla_attention",
    )(q, k, v)
    return out.reshape(batch * seq, WIDTH_A)


def _sb_kernel(q_ref, k_ref, v_ref, tri_ref, o_ref, za_ref, zb_ref, c_ref, acc_ref, *, tq, tk, tc):
    i = pl.program_id(2)
    n_full = (i * tq) // tk
    lane = lax.broadcasted_iota(jnp.int32, (1, LANES), 1)
    low = lane < HEAD_DIM
    c_ref[...] = jnp.zeros(c_ref.shape, F32)
    acc_ref[...] = jnp.zeros(acc_ref.shape, F32)
    sign = jnp.uint32(0x80000000)

    def logits(j, z_ref):
        start = pl.multiple_of(j * tk, tk)
        k = k_ref[pl.ds(start, tk), :]
        q = q_ref[...]
        zero = jnp.zeros_like(q)
        z_ref[0] = _dot_nt(jnp.where(low, q, zero), k)
        z_ref[1] = _dot_nt(jnp.where(low, zero, q), k)

    def absorb(j, z_ref, masked, width=tk):
        start = pl.multiple_of(j * tk, tk)
        v = v_ref[pl.ds(start, width), :]
        tri = tri_ref[...]
        if masked:
            diff = lax.broadcasted_iota(jnp.int32, (tq, width), 1) - lax.broadcasted_iota(jnp.int32, (tq, width), 0)
            keep = diff < i * tq - start
        staged = []
        for hh in range(2):
            z = z_ref[hh, :, :width]
            neg_abs = lax.bitcast_convert_type(lax.bitcast_convert_type(z, jnp.uint32) | sign, F32)
            sp = jnp.maximum(z, 0.0) + jnp.log2(1.0 + jnp.exp2(neg_abs))
            if masked:
                sp = jnp.where(keep, sp, 0.0)
            c = c_ref[hh]
            parts = []
            for cc in reversed(range(width // tc)):
                cs = slice(cc * tc, (cc + 1) * tc)
                spc = sp[:, cs]
                right = _dot(spc.astype(BF16), tri)
                parts.append((cs, z[:, cs] - spc, right, c))
                c = c + right[:, 0:1] + spc[:, 0:1]
            c_ref[hh] = c
            staged.append(parts)
        for hh in range(2):
            acc = acc_ref[hh]
            for cs, log_beta, right, c in staged[hh]:
                a = jnp.exp2(log_beta - right - c)
                if masked:
                    a = jnp.where(keep[:, cs], a, 0.0)
                acc = acc + _dot(a.astype(BF16), v[cs, :])
            acc_ref[hh] = acc

    def live():
        return jnp.min(c_ref[...]) < SB_DEAD_BITS

    def pairs(first_ref, second_ref):
        npair = n_full // 2

        def body(t, carry):
            top = 2 * (npair - t) - 1

            @pl.when(live())
            def _():
                logits(top - 1, second_ref)
                absorb(top, first_ref, False)

            @pl.when(live())
            def _():
                logits(jnp.maximum(top - 2, 0), first_ref)
                absorb(top - 1, second_ref, False)

            return carry

        lax.fori_loop(0, npair, body, 0)

    def absorb_diagonal(z_ref, next_ref):
        blocks = (i * tq - n_full * tk) // tq + 1
        for nb in range(1, tk // tq + 1):
            @pl.when(blocks == nb)
            def _():
                logits(jnp.maximum(n_full - 1, 0), next_ref)
                absorb(n_full, z_ref, True, nb * tq)

    logits(n_full, za_ref)
    absorb_diagonal(za_ref, zb_ref)

    @pl.when(n_full % 2 == 1)
    def _():
        @pl.when(live())
        def _():
            logits(jnp.maximum(n_full - 2, 0), za_ref)
            absorb(n_full - 1, zb_ref, False)

        pairs(za_ref, zb_ref)

    @pl.when(n_full % 2 == 0)
    def _():
        pairs(zb_ref, za_ref)

    o_ref[...] = jnp.where(low, acc_ref[0], acc_ref[1])


def _sb_attention(q, k, v, tri, batch, seq):
    tq, tk, tc = TQ_ATTN, TK_SB, TC_SB
    q = q.reshape(batch, seq, -1)
    k = k.reshape(batch, seq, -1)
    v = v.reshape(batch, seq, -1)
    out = pl.pallas_call(
        functools.partial(_sb_kernel, tq=tq, tk=tk, tc=tc),
        grid=(batch, SB_HEADS // 2, seq // tq),
        in_specs=[pl.BlockSpec((None, tq, LANES), lambda b, p, i: (b, i, p)),
                  pl.BlockSpec((None, seq, LANES), lambda b, p, i: (b, 0, p)),
                  pl.BlockSpec((None, seq, LANES), lambda b, p, i: (b, 0, p)),
                  pl.BlockSpec((tc, tc), lambda b, p, i: (0, 0))],
        out_specs=pl.BlockSpec((None, tq, LANES), lambda b, p, i: (b, i, p)),
        out_shape=jax.ShapeDtypeStruct((batch, seq, WIDTH_C), F32),
        scratch_shapes=[pltpu.VMEM((2, tq, tk), F32), pltpu.VMEM((2, tq, tk), F32),
                        pltpu.VMEM((2, tq, 1), F32), pltpu.VMEM((2, tq, LANES), F32)],
        compiler_params=pltpu.CompilerParams(dimension_semantics=("parallel", "parallel", "arbitrary"),
                                             vmem_limit_bytes=VMEM_LIMIT),
        name="sb_attention",
    )(q, k, v, tri)
    return out.reshape(batch * seq, WIDTH_C)


def _swa_kernel(slopes_ref, sinks_ref, q_ref, k_ref, v_ref, o_ref, *, tq):
    w = SWA_WINDOW
    group = SWA_Q_HEADS // SWA_KV_HEADS
    hkv = pl.program_id(1)
    i = pl.program_id(2)
    lane = lax.broadcasted_iota(jnp.int32, (1, LANES), 1)
    low = lane < HEAD_DIM
    row = lax.broadcasted_iota(jnp.int32, (w, 2 * w), 0)
    col = lax.broadcasted_iota(jnp.int32, (w, 2 * w), 1)
    for n in range(tq // w):
        q0 = i * tq + n * w
        ks = pl.multiple_of(jnp.maximum(q0 - w, 0), w)
        k = k_ref[pl.ds(ks, 2 * w), :]
        v = v_ref[pl.ds(ks, 2 * w), :]
        dist = (q0 - ks) + row - col
        valid = (dist >= 0) & (dist < w)
        distf = dist.astype(F32)
        for pr in range(group // 2):
            qp = q_ref[n * w:(n + 1) * w, LANES * pr:LANES * (pr + 1)]
            zero = jnp.zeros_like(qp)
            outs = []
            for half in range(2):
                head = hkv * group + pr * 2 + half
                qh = jnp.where(low, qp, zero) if half == 0 else jnp.where(low, zero, qp)
                s = _dot_nt(qh, k) - slopes_ref[head] * distf
                s = jnp.where(valid, s, NEG)
                sink = sinks_ref[head]
                m = jnp.maximum(jnp.max(s, axis=-1, keepdims=True), sink)
                p = jnp.exp(s - m)
                den = jnp.sum(p, axis=-1, keepdims=True) + jnp.exp(sink - m)
                outs.append(_dot(p.astype(BF16), v) / den)
            o_ref[n * w:(n + 1) * w, LANES * pr:LANES * (pr + 1)] = jnp.where(low, outs[0], outs[1])


def _swa_attention(q, k, v, slopes, sinks, batch, seq):
    tq = TQ_SWA
    q = q.reshape(batch, seq, -1)
    k = k.reshape(batch, seq, -1)
    v = v.reshape(batch, seq, -1)
    smem = pl.BlockSpec(memory_space=pltpu.SMEM)
    out = pl.pallas_call(
        functools.partial(_swa_kernel, tq=tq),
        grid=(batch, SWA_KV_HEADS, seq // tq),
        in_specs=[smem, smem,
                  pl.BlockSpec((None, tq, 2 * LANES), lambda b, h, i: (b, i, h)),
                  pl.BlockSpec((None, seq, LANES), lambda b, h, i: (b, 0, h)),
                  pl.BlockSpec((None, seq, LANES), lambda b, h, i: (b, 0, h))],
        out_specs=pl.BlockSpec((None, tq, 2 * LANES), lambda b, h, i: (b, i, h)),
        out_shape=jax.ShapeDtypeStruct((batch, seq, WIDTH_B), F32),
        compiler_params=pltpu.CompilerParams(dimension_semantics=("parallel", "parallel", "arbitrary"),
                                             vmem_limit_bytes=VMEM_LIMIT),
        name="swa_attention",
    )(slopes, sinks, q, k, v)
    return out.reshape(batch * seq, WIDTH_B)


def _out_proj_kernel(*refs, moe):
    if moe:
        oa, ob, oc, x_ref, gout, wout, gffn, wr_hi, wr_lo, xo_ref, h_ref, sel_ref = refs
    else:
        oa, ob, oc, x_ref, gout, wout, gffn, xo_ref, h_ref = refs
    g = gout[...]
    b0, b1 = WIDTH_A, WIDTH_A + WIDTH_B
    na = _rms(oa[...], g[:, :b0]).astype(BF16)
    nb = _rms(ob[...], g[:, b0:b1]).astype(BF16)
    nc = _rms(oc[...], g[:, b1:]).astype(BF16)
    xn = x_ref[...] + _dot(na, wout[:b0, :]) + _dot(nb, wout[b0:b1, :]) + _dot(nc, wout[b1:, :])
    xo_ref[...] = xn
    hf = _rms(xn, gffn[...])
    hb = hf.astype(BF16)
    h_ref[...] = hf if moe else hb
    if moe:
        lo = (hf - hb.astype(F32)).astype(BF16)
        logits = _dot(hb, wr_hi[...]) + _dot(lo, wr_hi[...]) + _dot(hb, wr_lo[...])
        lane = lax.broadcasted_iota(jnp.int32, logits.shape, 1)
        lg = jnp.where(lane < N_EXPERTS, logits, -jnp.inf)
        m1 = jnp.max(lg, axis=-1, keepdims=True)
        i1 = jnp.min(jnp.where(lg == m1, lane, LANES), axis=-1, keepdims=True)
        lg2 = jnp.where(lane == i1, -jnp.inf, lg)
        m2 = jnp.max(lg2, axis=-1, keepdims=True)
        i2 = jnp.min(jnp.where(lg2 == m2, lane, LANES), axis=-1, keepdims=True)
        e = jnp.exp(m2 - m1)
        g1 = 1.0 / (1.0 + e)
        g2 = e / (1.0 + e)
        sel_ref[...] = jnp.where(lane == 0, i1.astype(F32),
                                 jnp.where(lane == 1, i2.astype(F32),
                                           jnp.where(lane == 2, g1, jnp.where(lane == 3, g2, 0.0))))


def _out_proj(oa, ob, oc, x, lw, moe):
    n = x.shape[0]
    tm = TM_PROJ
    row = lambda i: (i, 0)
    fixed = lambda i: (0, 0)
    full = lambda a: pl.BlockSpec(a.shape, fixed)
    args = [oa, ob, oc, x, lw["out_g"], lw["w_out"], lw["ffn_g"]]
    in_specs = [pl.BlockSpec((tm, WIDTH_A), row), pl.BlockSpec((tm, WIDTH_B), row), pl.BlockSpec((tm, WIDTH_C), row),
                pl.BlockSpec((tm, D_MODEL), row), full(lw["out_g"]), full(lw["w_out"]), full(lw["ffn_g"])]
    out_specs = [pl.BlockSpec((tm, D_MODEL), row), pl.BlockSpec((tm, D_MODEL), row)]
    out_shape = [jax.ShapeDtypeStruct((n, D_MODEL), F32), jax.ShapeDtypeStruct((n, D_MODEL), F32 if moe else BF16)]
    if moe:
        args += [lw["router_hi"], lw["router_lo"]]
        in_specs += [full(lw["router_hi"]), full(lw["router_lo"])]
        out_specs.append(pl.BlockSpec((tm, LANES), row))
        out_shape.append(jax.ShapeDtypeStruct((n, LANES), F32))
    return pl.pallas_call(
        functools.partial(_out_proj_kernel, moe=moe),
        grid=(n // tm,),
        in_specs=in_specs, out_specs=out_specs, out_shape=out_shape,
        compiler_params=pltpu.CompilerParams(dimension_semantics=("parallel",), vmem_limit_bytes=VMEM_LIMIT),
        name="out_proj_moe" if moe else "out_proj",
    )(*args)


def _swiglu_partial(h, wg, wu, wd):
    h = h.astype(BF16)
    g = _dot(h, wg)
    u = _dot(h, wu)
    return _dot((g * jax.nn.sigmoid(g) * u).astype(BF16), wd)


def _ffn_kernel(h_ref, wg_ref, wu_ref, wd_ref, x_ref, o_ref):
    j = pl.program_id(1)
    y = _swiglu_partial(h_ref[...], wg_ref[...], wu_ref[...], wd_ref[...])

    @pl.when(j == 0)
    def _():
        o_ref[...] = x_ref[...] + y

    @pl.when(j > 0)
    def _():
        o_ref[...] += y


def _dense_ffn(h, x, lw):
    n = x.shape[0]
    tm, tf = TM_FFN, TF_FFN
    return pl.pallas_call(
        _ffn_kernel,
        grid=(n // tm, D_FF // tf),
        in_specs=[pl.BlockSpec((tm, D_MODEL), lambda i, j: (i, 0)),
                  pl.BlockSpec((D_MODEL, tf), lambda i, j: (0, j)),
                  pl.BlockSpec((D_MODEL, tf), lambda i, j: (0, j)),
                  pl.BlockSpec((tf, D_MODEL), lambda i, j: (j, 0)),
                  pl.BlockSpec((tm, D_MODEL), lambda i, j: (i, 0))],
        out_specs=pl.BlockSpec((tm, D_MODEL), lambda i, j: (i, 0)),
        out_shape=jax.ShapeDtypeStruct((n, D_MODEL), F32),
        compiler_params=pltpu.CompilerParams(dimension_semantics=("parallel", "arbitrary"),
                                             vmem_limit_bytes=VMEM_LIMIT),
        name="dense_ffn",
    )(h, lw["w_gate"], lw["w_up"], lw["w_down"], x)


def _moe_kernel(blk_e_ref, n_used_ref, xs_ref, wg_ref, wu_ref, wd_ref, o_ref):
    i = pl.program_id(0)
    j = pl.program_id(1)
    used = i < n_used_ref[0]

    @pl.when(used)
    def _():
        y = _swiglu_partial(xs_ref[...], wg_ref[...], wu_ref[...], wd_ref[...])

        @pl.when(j == 0)
        def _():
            o_ref[...] = y

        @pl.when(j > 0)
        def _():
            o_ref[...] += y

    @pl.when(jnp.logical_not(used) & (j == 0))
    def _():
        o_ref[...] = jnp.zeros_like(o_ref)


def _moe_experts(xs, blk_e, n_used, lw):
    p = xs.shape[0]
    tm, tf = TM_MOE, TF_MOE
    n_ff = D_FF_EXPERT // tf

    def col(i, j):
        return jnp.where(i % 2 == 0, j, n_ff - 1 - j)

    grid_spec = pltpu.PrefetchScalarGridSpec(
        num_scalar_prefetch=2,
        grid=(p // tm, n_ff),
        in_specs=[pl.BlockSpec((tm, D_MODEL), lambda i, j, be, nu: (i, 0)),
                  pl.BlockSpec((None, D_MODEL, tf), lambda i, j, be, nu: (be[i], 0, col(i, j))),
                  pl.BlockSpec((None, D_MODEL, tf), lambda i, j, be, nu: (be[i], 0, col(i, j))),
                  pl.BlockSpec((None, tf, D_MODEL), lambda i, j, be, nu: (be[i], col(i, j), 0))],
        out_specs=pl.BlockSpec((tm, D_MODEL), lambda i, j, be, nu: (i, 0)),
    )
    return pl.pallas_call(
        _moe_kernel,
        grid_spec=grid_spec,
        out_shape=jax.ShapeDtypeStruct((p, D_MODEL), F32),
        compiler_params=pltpu.CompilerParams(dimension_semantics=("arbitrary", "arbitrary"),
                                             vmem_limit_bytes=VMEM_LIMIT),
        name="moe_experts",
    )(blk_e, n_used, xs, lw["w_gate"], lw["w_up"], lw["w_down"])


def _row_copy(src_ref, src_row, dst_ref, dst_row, sem):
    return pltpu.make_async_copy(src_ref.at[pl.ds(src_row, 1)], dst_ref.at[pl.ds(dst_row, 1)], sem)


def _dispatch_kernel(dest_ref, h_ref, xs_in_ref, xs_ref, sem, *, tm):
    del xs_in_ref

    def start(r, carry):
        for k in range(2):
            _row_copy(h_ref, r, xs_ref, dest_ref[0, 2 * r + k], sem).start()
        return carry

    def wait(r, carry):
        for k in range(2):
            _row_copy(h_ref, r, xs_ref, dest_ref[0, 2 * r + k], sem).wait()
        return carry

    lax.fori_loop(0, tm, start, 0, unroll=8)
    lax.fori_loop(0, tm, wait, 0, unroll=8)


def _dispatch(dest_blk, h, p):
    n = h.shape[0]
    tm = TM_ROUTE
    xs0 = jnp.zeros((p, D_MODEL), F32)
    return pl.pallas_call(
        functools.partial(_dispatch_kernel, tm=tm),
        grid=(n // tm,),
        in_specs=[pl.BlockSpec((None, 1, 2 * tm), lambda i: (i, 0, 0), memory_space=pltpu.SMEM),
                  pl.BlockSpec((tm, D_MODEL), lambda i: (i, 0)),
                  pl.BlockSpec(memory_space=pl.ANY)],
        out_specs=pl.BlockSpec(memory_space=pl.ANY),
        out_shape=jax.ShapeDtypeStruct((p, D_MODEL), F32),
        scratch_shapes=[pltpu.SemaphoreType.DMA(())],
        input_output_aliases={2: 0},
        compiler_params=pltpu.CompilerParams(dimension_semantics=("arbitrary",), vmem_limit_bytes=VMEM_LIMIT,
                                             has_side_effects=True),
        name="moe_dispatch",
    )(dest_blk, h, xs0)


def _combine_kernel(dest_ref, ys_ref, x_ref, sel_ref, o_ref, ybuf, sem, *, tm):
    def start(r, carry):
        for k in range(2):
            _row_copy(ys_ref, dest_ref[0, 2 * r + k], ybuf.at[k], r, sem).start()
        return carry

    def wait(r, carry):
        for k in range(2):
            _row_copy(ys_ref, dest_ref[0, 2 * r + k], ybuf.at[k], r, sem).wait()
        return carry

    lax.fori_loop(0, tm, start, 0, unroll=8)
    lax.fori_loop(0, tm, wait, 0, unroll=8)
    sel = sel_ref[...]
    o_ref[...] = x_ref[...] + ybuf[0] * sel[:, 2:3] + ybuf[1] * sel[:, 3:4]


def _combine(dest_blk, ys, x, sel):
    n = x.shape[0]
    tm = TM_ROUTE
    return pl.pallas_call(
        functools.partial(_combine_kernel, tm=tm),
        grid=(n // tm,),
        in_specs=[pl.BlockSpec((None, 1, 2 * tm), lambda i: (i, 0, 0), memory_space=pltpu.SMEM),
                  pl.BlockSpec(memory_space=pl.ANY),
                  pl.BlockSpec((tm, D_MODEL), lambda i: (i, 0)),
                  pl.BlockSpec((tm, LANES), lambda i: (i, 0))],
        out_specs=pl.BlockSpec((tm, D_MODEL), lambda i: (i, 0)),
        out_shape=jax.ShapeDtypeStruct((n, D_MODEL), F32),
        scratch_shapes=[pltpu.VMEM((2, tm, D_MODEL), F32), pltpu.SemaphoreType.DMA(())],
        compiler_params=pltpu.CompilerParams(dimension_semantics=("arbitrary",), vmem_limit_bytes=VMEM_LIMIT),
        name="moe_combine",
    )(dest_blk, ys, x, sel)


def _moe_ffn(h, x, sel, lw):
    n = x.shape[0]
    tm = TM_MOE
    top_idx = sel[:, :2].astype(jnp.int32)
    flat_e = top_idx.reshape(-1)
    onehot = (flat_e[:, None] == jnp.arange(N_EXPERTS, dtype=jnp.int32)[None, :]).astype(jnp.int32)
    csum = jnp.cumsum(onehot, axis=0)
    rank = jnp.sum(csum * onehot, axis=1) - 1
    counts = csum[-1]
    padded = (counts + tm - 1) // tm * tm
    pend = jnp.cumsum(padded)
    pstart = pend - padded
    dest = pstart[flat_e] + rank
    p = n * 2 + N_EXPERTS * tm
    n_blk = p // tm
    blk_start = jnp.arange(n_blk, dtype=jnp.int32) * tm
    blk_e = jnp.minimum(jnp.sum((pend[None, :] <= blk_start[:, None]).astype(jnp.int32), axis=1), N_EXPERTS - 1)
    n_used = (pend[-1:] // tm).astype(jnp.int32)
    dest_blk = dest.astype(jnp.int32).reshape(n // TM_ROUTE, 1, 2 * TM_ROUTE)
    xs = _dispatch(dest_blk, h, p)
    ys = _moe_experts(xs, blk_e, n_used, lw)
    return _combine(dest_blk, ys, x, sel)


def _pad_cols(w, width):
    return jnp.pad(w, ((0, 0), (0, width - w.shape[1])))


def _head_groups(w, heads, dim):
    kdim = w.shape[0]
    return jnp.pad(w.reshape(kdim, heads, dim), ((0, 0), (0, 0), (0, LANES - dim))).reshape(kdim, heads * LANES)


def _layer_weights(layer, p):
    w_in = p["w_in"][layer]
    a, b, c = w_in[:, :COLS_A], w_in[:, COLS_A:COLS_A + COLS_B], w_in[:, COLS_A + COLS_B:]
    kpe = jnp.pad(a[:, MLA_Q_RANK + MLA_KV_RANK:], ((0, 0), (MLA_NOPE, LANES - MLA_QK)))
    bq = b[:, :WIDTH_B]

    def dup(w):
        return jnp.repeat(w.reshape(D_MODEL, SWA_KV_HEADS, 1, HEAD_DIM), 2, axis=2).reshape(D_MODEL, 2 * LANES)

    bk = dup(b[:, WIDTH_B:WIDTH_B + SWA_KV_HEADS * HEAD_DIM])
    bv = dup(b[:, WIDTH_B + SWA_KV_HEADS * HEAD_DIM:])
    w_in_p = jnp.concatenate([a[:, :MLA_Q_RANK + MLA_KV_RANK], kpe, bq, bk, bv, c], axis=1).astype(BF16)
    w_ukv = p["mla_w_ukv"][layer].reshape(MLA_KV_RANK, MLA_HEADS, MLA_NOPE + MLA_V)
    w_uk = _head_groups(w_ukv[:, :, :MLA_NOPE].reshape(MLA_KV_RANK, -1), MLA_HEADS, MLA_NOPE)
    w_uv = w_ukv[:, :, MLA_NOPE:].reshape(MLA_KV_RANK, -1)
    row = lambda g: g.reshape(1, -1).astype(F32)
    lw = {
        "attn_g": row(p["attn_norm_g"][layer]),
        "w_in": w_in_p,
        "cq_g": row(p["mla_cq_norm_g"][layer]),
        "w_uq": _head_groups(p["mla_w_uq"][layer], MLA_HEADS, MLA_QK).astype(BF16),
        "ckv_g": row(p["mla_ckv_norm_g"][layer]),
        "w_ukv": jnp.concatenate([w_uk, w_uv], axis=1).astype(BF16),
        "mla_qg": _pad_cols(row(p["mla_qn_g"][layer]), LANES),
        "mla_kg": _pad_cols(row(p["mla_kn_g"][layer]), LANES),
        "swa_qg": jnp.tile(row(p["swa_qn_g"][layer]), (1, 2)),
        "swa_kg": jnp.tile(row(p["swa_kn_g"][layer]), (1, 2)),
        "sinks": p["swa_sinks"][layer].astype(F32),
        "out_g": row(p["out_norm_g"][layer]),
        "w_out": p["w_out"][layer].astype(BF16),
        "ffn_g": row(p["ffn_norm_g"][layer]),
    }
    j = layer // 2
    if layer % 2 == 0:
        lw["w_gate"] = p["dense_w_gate"][j].astype(BF16)
        lw["w_up"] = p["dense_w_up"][j].astype(BF16)
        lw["w_down"] = p["dense_w_down"][j].astype(BF16)
    else:
        rw = _pad_cols(p["router_w"][j], LANES)
        hi = rw.astype(BF16)
        lw["router_hi"] = hi
        lw["router_lo"] = (rw - hi.astype(F32)).astype(BF16)
        lw["w_gate"] = p["moe_w_gate"][j].astype(BF16)
        lw["w_up"] = p["moe_w_up"][j].astype(BF16)
        lw["w_down"] = p["moe_w_down"][j].astype(BF16)
    return lw


def _rope_tables(seq):
    half = MLA_ROPE // 2
    inv = ROPE_THETA ** (-jnp.arange(half, dtype=F32) / half)
    ang = jnp.arange(seq).astype(F32)[:, None] * inv[None, :]
    cos, sin = jnp.cos(ang), jnp.sin(ang)
    zl = jnp.zeros((seq, MLA_NOPE), F32)
    zh = jnp.zeros((seq, half), F32)
    zr = jnp.zeros((seq, LANES - MLA_QK), F32)
    cos_t = jnp.concatenate([zl + 1.0, cos, cos, zr + 1.0], axis=1)
    sina = jnp.concatenate([zl, -sin, zh, zr], axis=1)
    sinb = jnp.concatenate([zl, zh, sin, zr], axis=1)
    return cos_t, sina, sinb


def kernel(x, attn_norm_g, w_in, mla_cq_norm_g, mla_w_uq, mla_ckv_norm_g, mla_w_ukv, mla_qn_g, mla_kn_g, swa_qn_g, swa_kn_g, swa_sinks, out_norm_g, w_out, ffn_norm_g, dense_w_gate, dense_w_up, dense_w_down, router_w, moe_w_gate, moe_w_up, moe_w_down):
    params = dict(attn_norm_g=attn_norm_g, w_in=w_in, mla_cq_norm_g=mla_cq_norm_g, mla_w_uq=mla_w_uq,
                  mla_ckv_norm_g=mla_ckv_norm_g, mla_w_ukv=mla_w_ukv, mla_qn_g=mla_qn_g, mla_kn_g=mla_kn_g,
                  swa_qn_g=swa_qn_g, swa_kn_g=swa_kn_g, swa_sinks=swa_sinks, out_norm_g=out_norm_g, w_out=w_out,
                  ffn_norm_g=ffn_norm_g, dense_w_gate=dense_w_gate, dense_w_up=dense_w_up,
                  dense_w_down=dense_w_down, router_w=router_w, moe_w_gate=moe_w_gate, moe_w_up=moe_w_up,
                  moe_w_down=moe_w_down)
    batch, seq, _ = x.shape
    depth = w_in.shape[0]
    tabs = _rope_tables(seq)
    slopes = jnp.exp2(-8.0 * jnp.arange(1, SWA_Q_HEADS + 1, dtype=F32) / SWA_Q_HEADS)
    tri = (jnp.arange(TC_SB)[:, None] > jnp.arange(TC_SB)[None, :]).astype(BF16)
    xf =x.reshape(batch * seq, D_MODEL)
    for layer in range(depth):
        lw = _layer_weights(layer, params)
        mq, mk, mv, sq, sk, sv, cq, ck, cv = _in_proj(xf, lw, tabs, seq)
        o_a = _mla_attention(mq, mk, mv, batch, seq)
        o_b = _swa_attention(sq, sk, sv, slopes, lw["sinks"], batch, seq)
        o_c = _sb_attention(cq, ck, cv, tri, batch, seq)
        if layer % 2 == 0:
            xf, h = _out_proj(o_a, o_b, o_c, xf, lw, moe=False)
            xf = _dense_ffn(h, xf, lw)
        else:
            xf, h, sel = _out_proj(o_a, o_b, o_c, xf, lw, moe=True)
            xf = _moe_ffn(h, xf, sel, lw)
    return xf.reshape(batch, seq, D_MODEL)
```

```python
import functools

import jax
import jax.numpy as jnp
from jax import lax
from jax.experimental import pallas as pl
from jax.experimental.pallas import tpu as pltpu

F32 = jnp.float32
BF16 = jnp.bfloat16

D_MODEL = 1024
HEAD_DIM = 64
MLA_HEADS = 4
MLA_Q_RANK = 256
MLA_KV_RANK = 128
MLA_NOPE = 64
MLA_ROPE = 32
MLA_QK = MLA_NOPE + MLA_ROPE
MLA_V = 64
ROPE_THETA = 10000.0
SWA_Q_HEADS = 8
SWA_KV_HEADS = 2
SWA_WINDOW = 128
SB_HEADS = 4
WIDTH_A = MLA_HEADS * MLA_V
WIDTH_B = SWA_Q_HEADS * HEAD_DIM
WIDTH_C = SB_HEADS * HEAD_DIM
COLS_A = MLA_Q_RANK + MLA_KV_RANK + MLA_ROPE
COLS_B = (SWA_Q_HEADS + 2 * SWA_KV_HEADS) * HEAD_DIM
D_FF = 2816
N_EXPERTS = 8
D_FF_EXPERT = 3584
RMS_EPS = 1e-6
NEG = -1e30
LOG2E = 1.4426950408889634

LANES = 128
VMEM_LIMIT = 56 * 1024 * 1024

TM_PROJ = 512
TM_IN = 1024
TQ_ATTN = 256
TK_ATTN = 1024
TK_SB = 256
TC_SB = 256
SB_DEAD_BITS = 151.0
TQ_SWA = 512
TM_FFN = 512
TF_FFN = 1408
TM_MOE = 512
TM_ROUTE = 512
TF_MOE = 1792

P_CQ = 0
P_CKV = 256
P_KPE = 384
P_SWQ = 512
P_SWK = 1024
P_SWV = 1280
P_SBQ = 1536
P_SBK = 1792
P_SBV = 2048
P_COLS = 2304


def _dot(a, b):
    return jnp.dot(a, b, preferred_element_type=F32)


def _dot_nt(a, b):
    return lax.dot_general(a, b, (((1,), (1,)), ((), ())), preferred_element_type=F32)


def _rms(x, g):
    return x * lax.rsqrt(jnp.mean(x * x, axis=-1, keepdims=True) + RMS_EPS) * g


def _in_proj_kernel(x_ref, g_ref, w_ref, gcq_ref, wuq_ref, gckv_ref, wukv_ref, gq_ref, gk_ref,
                    cos_ref, sina_ref, sinb_ref, sgq_ref, sgk_ref,
                    mq_ref, mk_ref, mv_ref, sq_ref, sk_ref, sv_ref, cq_ref, ck_ref, cv_ref, *, sub):
    lane = lax.broadcasted_iota(jnp.int32, (1, LANES), 1)
    low = lane < HEAD_DIM
    scale = HEAD_DIM ** -0.5

    def head_norm(t, g):
        return t * lax.rsqrt(jnp.sum(t * t, axis=-1, keepdims=True) * (1.0 / MLA_QK) + RMS_EPS) * g

    def pair_norm(t, g):
        t2 = t * t
        lo = jnp.sum(jnp.where(low, t2, 0.0), axis=-1, keepdims=True)
        hi = jnp.sum(jnp.where(low, 0.0, t2), axis=-1, keepdims=True)
        ms = jnp.where(low, lo, hi) * (1.0 / HEAD_DIM)
        return t * lax.rsqrt(ms + RMS_EPS) * g

    def rows(rs):
        h = _rms(x_ref[rs, :], g_ref[...]).astype(BF16)

        def proj(lo, hi):
            return _dot(h, w_ref[:, lo:hi])

        cos = cos_ref[rs, :]
        sina = sina_ref[rs, :]
        sinb = sinb_ref[rs, :]

        def rope(t):
            return t * cos + pltpu.roll(t, LANES - MLA_ROPE // 2, 1) * sina + pltpu.roll(t, MLA_ROPE // 2, 1) * sinb

        a = proj(P_CQ, P_SWQ)
        q = _dot(_rms(a[:, :MLA_Q_RANK], gcq_ref[...]).astype(BF16), wuq_ref[...])
        for hh in range(MLA_HEADS):
            sl = slice(LANES * hh, LANES * (hh + 1))
            mq_ref[rs, sl] = (rope(head_norm(q[:, sl], gq_ref[...])) * (MLA_QK ** -0.5 * LOG2E)).astype(BF16)
        kv = _dot(_rms(a[:, P_CKV:P_KPE], gckv_ref[...]).astype(BF16), wukv_ref[...])
        kpe = a[:, P_KPE:P_SWQ]
        for hh in range(MLA_HEADS):
            sl = slice(LANES * hh, LANES * (hh + 1))
            mk_ref[rs, sl] = rope(head_norm(kv[:, sl] + kpe, gk_ref[...])).astype(BF16)
        mv_ref[rs, :] = kv[:, MLA_HEADS * LANES:].astype(BF16)
        bq = proj(P_SWQ, P_SWK)
        for gi in range(WIDTH_B // LANES):
            sl = slice(LANES * gi, LANES * (gi + 1))
            sq_ref[rs, sl] = (pair_norm(bq[:, sl], sgq_ref[...]) * scale).astype(BF16)
        bk = proj(P_SWK, P_SWV)
        for gi in range(SWA_KV_HEADS):
            sl = slice(LANES * gi, LANES * (gi + 1))
            sk_ref[rs, sl] = pair_norm(bk[:, sl], sgk_ref[...]).astype(BF16)
        sv_ref[rs, :] = proj(P_SWV, P_SBQ).astype(BF16)
        cq_ref[rs, :] = (proj(P_SBQ, P_SBK) * (scale * LOG2E)).astype(BF16)
        ck_ref[rs, :] = proj(P_SBK, P_SBV).astype(BF16)
        cv_ref[rs, :] = proj(P_SBV, P_COLS).astype(BF16)

    for r0 in range(0, x_ref.shape[0], sub):
        rows(slice(r0, r0 + sub))


def _in_proj(x, lw, tabs, seq):
    n = x.shape[0]
    tm = TM_IN
    nt = seq // tm
    row = lambda i: (i, 0)
    fixed = lambda i: (0, 0)
    pos = lambda i: (i % nt, 0)
    full = lambda a: pl.BlockSpec(a.shape, fixed)
    cos, sina, sinb = tabs
    widths = (4 * LANES, 4 * LANES, WIDTH_A, WIDTH_B, 2 * LANES, 2 * LANES, WIDTH_C, WIDTH_C, WIDTH_C)
    return pl.pallas_call(
        functools.partial(_in_proj_kernel, sub=TM_PROJ),
        grid=(n // tm,),
        in_specs=[pl.BlockSpec((tm, D_MODEL), row), full(lw["attn_g"]), full(lw["w_in"]), full(lw["cq_g"]),
                  full(lw["w_uq"]), full(lw["ckv_g"]), full(lw["w_ukv"]), full(lw["mla_qg"]), full(lw["mla_kg"]),
                  pl.BlockSpec((tm, LANES), pos), pl.BlockSpec((tm, LANES), pos), pl.BlockSpec((tm, LANES), pos),
                  full(lw["swa_qg"]), full(lw["swa_kg"])],
        out_specs=[pl.BlockSpec((tm, w), row) for w in widths],
        out_shape=[jax.ShapeDtypeStruct((n, w), BF16) for w in widths],
        compiler_params=pltpu.CompilerParams(dimension_semantics=("parallel",), vmem_limit_bytes=VMEM_LIMIT),
        name="in_proj",
    )(x, lw["attn_g"], lw["w_in"], lw["cq_g"], lw["w_uq"], lw["ckv_g"], lw["w_ukv"], lw["mla_qg"], lw["mla_kg"],
      cos, sina, sinb, lw["swa_qg"], lw["swa_kg"])


def _mla_kernel(q_ref, k_ref, v_ref, o_ref, sa_ref, sb_ref, m_ref, l_ref, acc_ref, *, tq, tk):
    i = pl.program_id(2)
    n_full = (i * tq) // tk
    m_ref[...] = jnp.full(m_ref.shape, NEG, F32)
    l_ref[...] = jnp.zeros(l_ref.shape, F32)
    acc_ref[...] = jnp.zeros(acc_ref.shape, F32)

    def scores(j, s_ref):
        start = pl.multiple_of(j * tk, tk)
        k = k_ref[pl.ds(start, tk), :]
        for hh in range(2):
            sl = slice(LANES * hh, LANES * (hh + 1))
            s_ref[hh] = _dot_nt(q_ref[:, sl], k[:, sl])

    def absorb(j, s_ref, masked, width=tk):
        start = pl.multiple_of(j * tk, tk)
        v = v_ref[pl.ds(start, width), :]
        if masked:
            diff = lax.broadcasted_iota(jnp.int32, (tq, width), 1) - lax.broadcasted_iota(jnp.int32, (tq, width), 0)
            keep = diff <= i * tq - start
        for hh in range(2):
            s = s_ref[hh, :, :width]
            if masked:
                s = jnp.where(keep, s, NEG)
            m = m_ref[hh]
            m_new = jnp.maximum(m, jnp.max(s, axis=-1, keepdims=True))
            alpha = jnp.exp2(m - m_new)
            p = jnp.exp2(s - m_new)
            l_ref[hh] = alpha * l_ref[hh] + jnp.sum(p, axis=-1, keepdims=True)
            acc_ref[hh] = alpha * acc_ref[hh] + _dot(p.astype(BF16), v)
            m_ref[hh] = m_new

    scores(0, sa_ref)

    def pair(t, carry):
        scores(2 * t + 1, sb_ref)
        absorb(2 * t, sa_ref, False)
        scores(2 * t + 2, sa_ref)
        absorb(2 * t + 1, sb_ref, False)
        return carry

    lax.fori_loop(0, n_full // 2, pair, 0)
    base = 2 * (n_full // 2)

    def absorb_diagonal(s_ref):
        blocks = (i * tq - n_full * tk) // tq + 1
        for nb in range(1, tk // tq + 1):
            pl.when(blocks == nb)(functools.partial(absorb, n_full, s_ref, True, nb * tq))

    @pl.when(n_full % 2 == 1)
    def _():
        scores(base + 1, sb_ref)
        absorb(base, sa_ref, False)
        absorb_diagonal(sb_ref)

    @pl.when(n_full % 2 == 0)
    def _():
        absorb_diagonal(sa_ref)

    lane = lax.broadcasted_iota(jnp.int32, (1, LANES), 1)
    o_ref[...] = jnp.where(lane < MLA_V, acc_ref[0] / l_ref[0], acc_ref[1] / l_ref[1])


def _mla_attention(q, k, v, batch, seq):
    tq, tk = TQ_ATTN, TK_ATTN
    q = q.reshape(batch, seq, -1)
    k = k.reshape(batch, seq, -1)
    v = v.reshape(batch, seq, -1)
    out = pl.pallas_call(
        functools.partial(_mla_kernel, tq=tq, tk=tk),
        grid=(batch, MLA_HEADS // 2, seq // tq),
        in_specs=[pl.BlockSpec((None, tq, 2 * LANES), lambda b, p, i: (b, i, p)),
                  pl.BlockSpec((None, seq, 2 * LANES), lambda b, p, i: (b, 0, p)),
                  pl.BlockSpec((None, seq, LANES), lambda b, p, i: (b, 0, p))],
        out_specs=pl.BlockSpec((None, tq, LANES), lambda b, p, i: (b, i, p)),
        out_shape=jax.ShapeDtypeStruct((batch, seq, WIDTH_A), F32),
        scratch_shapes=[pltpu.VMEM((2, tq, tk), F32), pltpu.VMEM((2, tq, tk), F32),
                        pltpu.VMEM((2, tq, 1), F32), pltpu.VMEM((2, tq, 1), F32), pltpu.VMEM((2, tq, LANES), F32)],
        compiler_params=pltpu.CompilerParams(dimension_semantics=("parallel", "parallel", "arbitrary"),
                                             vmem_limit_bytes=VMEM_LIMIT),
        name="mla_attention",
    )(q, k, v)
    return out.reshape(batch * seq, WIDTH_A)


def _sb_kernel(q_ref, k_ref, v_ref, tri_ref, o_ref, za_ref, zb_ref, c_ref, acc_ref, *, tq, tk, tc):
    i = pl.program_id(2)
    n_full = (i * tq) // tk
    lane = lax.broadcasted_iota(jnp.int32, (1, LANES), 1)
    low = lane < HEAD_DIM
    c_ref[...] = jnp.zeros(c_ref.shape, F32)
    acc_ref[...] = jnp.zeros(acc_ref.shape, F32)
    sign = jnp.uint32(0x80000000)

    def logits(j, z_ref):
        start = pl.multiple_of(j * tk, tk)
        k = k_ref[pl.ds(start, tk), :]
        q = q_ref[...]
        zero = jnp.zeros_like(q)
        z_ref[0] = _dot_nt(jnp.where(low, q, zero), k)
        z_ref[1] = _dot_nt(jnp.where(low, zero, q), k)

    def absorb(j, z_ref, masked, width=tk):
        start = pl.multiple_of(j * tk, tk)
        v = v_ref[pl.ds(start, width), :]
        tri = tri_ref[...]
        if masked:
            diff = lax.broadcasted_iota(jnp.int32, (tq, width), 1) - lax.broadcasted_iota(jnp.int32, (tq, width), 0)
            keep = diff < i * tq - start
        staged = []
        for hh in range(2):
            z = z_ref[hh, :, :width]
            neg_abs = lax.bitcast_convert_type(lax.bitcast_convert_type(z, jnp.uint32) | sign, F32)
            sp = jnp.maximum(z, 0.0) + jnp.log2(1.0 + jnp.exp2(neg_abs))
            if masked:
                sp = jnp.where(keep, sp, 0.0)
            c = c_ref[hh]
            parts = []
            for cc in reversed(range(width // tc)):
                cs = slice(cc * tc, (cc + 1) * tc)
                spc = sp[:, cs]
                right = _dot(spc.astype(BF16), tri)
                parts.append((cs, z[:, cs] - spc, right, c))
                c = c + right[:, 0:1] + spc[:, 0:1]
            c_ref[hh] = c
            staged.append(parts)
        for hh in range(2):
            acc = acc_ref[hh]
            for cs, log_beta, right, c in staged[hh]:
                a = jnp.exp2(log_beta - right - c)
                if masked:
                    a = jnp.where(keep[:, cs], a, 0.0)
                acc = acc + _dot(a.astype(BF16), v[cs, :])
            acc_ref[hh] = acc

    def live():
        return jnp.min(c_ref[...]) < SB_DEAD_BITS

    def pairs(first_ref, second_ref):
        npair = n_full // 2

        def cond(carry):
            t, alive = carry
            return jnp.logical_and(t < npair, alive)

        def body(carry):
            t, _ = carry
            top = 2 * (npair - t) - 1
            logits(top - 1, second_ref)
            absorb(top, first_ref, False)

            @pl.when(live())
            def _():
                logits(jnp.maximum(top - 2, 0), first_ref)
                absorb(top - 1, second_ref, False)

            return t + 1, live()

        lax.while_loop(cond, body, (jnp.int32(0), live()))

    def absorb_diagonal(z_ref, next_ref):
        blocks = (i * tq - n_full * tk) // tq + 1
        for nb in range(1, tk // tq + 1):
            @pl.when(blocks == nb)
            def _():
                logits(jnp.maximum(n_full - 1, 0), next_ref)
                absorb(n_full, z_ref, True, nb * tq)

    logits(n_full, za_ref)
    absorb_diagonal(za_ref, zb_ref)

    @pl.when(n_full % 2 == 1)
    def _():
        @pl.when(live())
        def _():
            logits(jnp.maximum(n_full - 2, 0), za_ref)
            absorb(n_full - 1, zb_ref, False)

        pairs(za_ref, zb_ref)

    @pl.when(n_full % 2 == 0)
    def _():
        pairs(zb_ref, za_ref)

    o_ref[...] = jnp.where(low, acc_ref[0], acc_ref[1])


def _sb_attention(q, k, v, tri, batch, seq):
    tq, tk, tc = TQ_ATTN, TK_SB, TC_SB
    q = q.reshape(batch, seq, -1)
    k = k.reshape(batch, seq, -1)
    v = v.reshape(batch, seq, -1)
    out = pl.pallas_call(
        functools.partial(_sb_kernel, tq=tq, tk=tk, tc=tc),
        grid=(batch, SB_HEADS // 2, seq // tq),
        in_specs=[pl.BlockSpec((None, tq, LANES), lambda b, p, i: (b, i, p)),
                  pl.BlockSpec((None, seq, LANES), lambda b, p, i: (b, 0, p)),
                  pl.BlockSpec((None, seq, LANES), lambda b, p, i: (b, 0, p)),
                  pl.BlockSpec((tc, tc), lambda b, p, i: (0, 0))],
        out_specs=pl.BlockSpec((None, tq, LANES), lambda b, p, i: (b, i, p)),
        out_shape=jax.ShapeDtypeStruct((batch, seq, WIDTH_C), F32),
        scratch_shapes=[pltpu.VMEM((2, tq, tk), F32), pltpu.VMEM((2, tq, tk), F32),
                        pltpu.VMEM((2, tq, 1), F32), pltpu.VMEM((2, tq, LANES), F32)],
        compiler_params=pltpu.CompilerParams(dimension_semantics=("parallel", "parallel", "arbitrary"),
                                             vmem_limit_bytes=VMEM_LIMIT),
        name="sb_attention",
    )(q, k, v, tri)
    return out.reshape(batch * seq, WIDTH_C)


def _swa_kernel(slopes_ref, sinks_ref, q_ref, k_ref, v_ref, o_ref, *, tq):
    w = SWA_WINDOW
    group = SWA_Q_HEADS // SWA_KV_HEADS
    hkv = pl.program_id(1)
    i = pl.program_id(2)
    lane = lax.broadcasted_iota(jnp.int32, (1, LANES), 1)
    low = lane < HEAD_DIM
    row = lax.broadcasted_iota(jnp.int32, (w, 2 * w), 0)
    col = lax.broadcasted_iota(jnp.int32, (w, 2 * w), 1)
    for n in range(tq // w):
        q0 = i * tq + n * w
        ks = pl.multiple_of(jnp.maximum(q0 - w, 0), w)
        k = k_ref[pl.ds(ks, 2 * w), :]
        v = v_ref[pl.ds(ks, 2 * w), :]
        dist = (q0 - ks) + row - col
        valid = (dist >= 0) & (dist < w)
        distf = dist.astype(F32)
        for pr in range(group // 2):
            qp = q_ref[n * w:(n + 1) * w, LANES * pr:LANES * (pr + 1)]
            zero = jnp.zeros_like(qp)
            outs = []
            for half in range(2):
                head = hkv * group + pr * 2 + half
                qh = jnp.where(low, qp, zero) if half == 0 else jnp.where(low, zero, qp)
                s = _dot_nt(qh, k) - slopes_ref[head] * distf
                s = jnp.where(valid, s, NEG)
                sink = sinks_ref[head]
                m = jnp.maximum(jnp.max(s, axis=-1, keepdims=True), sink)
                p = jnp.exp(s - m)
                den = jnp.sum(p, axis=-1, keepdims=True) + jnp.exp(sink - m)
                outs.append(_dot(p.astype(BF16), v) / den)
            o_ref[n * w:(n + 1) * w, LANES * pr:LANES * (pr + 1)] = jnp.where(low, outs[0], outs[1])


def _swa_attention(q, k, v, slopes, sinks, batch, seq):
    tq = TQ_SWA
    q = q.reshape(batch, seq, -1)
    k = k.reshape(batch, seq, -1)
    v = v.reshape(batch, seq, -1)
    smem = pl.BlockSpec(memory_space=pltpu.SMEM)
    out = pl.pallas_call(
        functools.partial(_swa_kernel, tq=tq),
        grid=(batch, SWA_KV_HEADS, seq // tq),
        in_specs=[smem, smem,
                  pl.BlockSpec((None, tq, 2 * LANES), lambda b, h, i: (b, i, h)),
                  pl.BlockSpec((None, seq, LANES), lambda b, h, i: (b, 0, h)),
                  pl.BlockSpec((None, seq, LANES), lambda b, h, i: (b, 0, h))],
        out_specs=pl.BlockSpec((None, tq, 2 * LANES), lambda b, h, i: (b, i, h)),
        out_shape=jax.ShapeDtypeStruct((batch, seq, WIDTH_B), F32),
        compiler_params=pltpu.CompilerParams(dimension_semantics=("parallel", "parallel", "arbitrary"),
                                             vmem_limit_bytes=VMEM_LIMIT),
        name="swa_attention",
    )(slopes, sinks, q, k, v)
    return out.reshape(batch * seq, WIDTH_B)


def _out_proj_kernel(*refs, moe):
    if moe:
        oa, ob, oc, x_ref, gout, wout, gffn, wr_hi, wr_lo, xo_ref, h_ref, sel_ref = refs
    else:
        oa, ob, oc, x_ref, gout, wout, gffn, xo_ref, h_ref = refs
    g = gout[...]
    b0, b1 = WIDTH_A, WIDTH_A + WIDTH_B
    na = _rms(oa[...], g[:, :b0]).astype(BF16)
    nb = _rms(ob[...], g[:, b0:b1]).astype(BF16)
    nc = _rms(oc[...], g[:, b1:]).astype(BF16)
    xn = x_ref[...] + _dot(na, wout[:b0, :]) + _dot(nb, wout[b0:b1, :]) + _dot(nc, wout[b1:, :])
    xo_ref[...] = xn
    hf = _rms(xn, gffn[...])
    hb = hf.astype(BF16)
    h_ref[...] = hf if moe else hb
    if moe:
        lo = (hf - hb.astype(F32)).astype(BF16)
        logits = _dot(hb, wr_hi[...]) + _dot(lo, wr_hi[...]) + _dot(hb, wr_lo[...])
        lane = lax.broadcasted_iota(jnp.int32, logits.shape, 1)
        lg = jnp.where(lane < N_EXPERTS, logits, -jnp.inf)
        m1 = jnp.max(lg, axis=-1, keepdims=True)
        i1 = jnp.min(jnp.where(lg == m1, lane, LANES), axis=-1, keepdims=True)
        lg2 = jnp.where(lane == i1, -jnp.inf, lg)
        m2 = jnp.max(lg2, axis=-1, keepdims=True)
        i2 = jnp.min(jnp.where(lg2 == m2, lane, LANES), axis=-1, keepdims=True)
        e = jnp.exp(m2 - m1)
        g1 = 1.0 / (1.0 + e)
        g2 = e / (1.0 + e)
        sel_ref[...] = jnp.where(lane == 0, i1.astype(F32),
                                 jnp.where(lane == 1, i2.astype(F32),
                                           jnp.where(lane == 2, g1, jnp.where(lane == 3, g2, 0.0))))


def _out_proj(oa, ob, oc, x, lw, moe):
    n = x.shape[0]
    tm = TM_PROJ
    row = lambda i: (i, 0)
    fixed = lambda i: (0, 0)
    full = lambda a: pl.BlockSpec(a.shape, fixed)
    args = [oa, ob, oc, x, lw["out_g"], lw["w_out"], lw["ffn_g"]]
    in_specs = [pl.BlockSpec((tm, WIDTH_A), row), pl.BlockSpec((tm, WIDTH_B), row), pl.BlockSpec((tm, WIDTH_C), row),
                pl.BlockSpec((tm, D_MODEL), row), full(lw["out_g"]), full(lw["w_out"]), full(lw["ffn_g"])]
    out_specs = [pl.BlockSpec((tm, D_MODEL), row), pl.BlockSpec((tm, D_MODEL), row)]
    out_shape = [jax.ShapeDtypeStruct((n, D_MODEL), F32), jax.ShapeDtypeStruct((n, D_MODEL), F32 if moe else BF16)]
    if moe:
        args += [lw["router_hi"], lw["router_lo"]]
        in_specs += [full(lw["router_hi"]), full(lw["router_lo"])]
        out_specs.append(pl.BlockSpec((tm, LANES), row))
        out_shape.append(jax.ShapeDtypeStruct((n, LANES), F32))
    return pl.pallas_call(
        functools.partial(_out_proj_kernel, moe=moe),
        grid=(n // tm,),
        in_specs=in_specs, out_specs=out_specs, out_shape=out_shape,
        compiler_params=pltpu.CompilerParams(dimension_semantics=("parallel",), vmem_limit_bytes=VMEM_LIMIT),
        name="out_proj_moe" if moe else "out_proj",
    )(*args)


def _swiglu_partial(h, wg, wu, wd):
    h = h.astype(BF16)
    g = _dot(h, wg)
    u = _dot(h, wu)
    return _dot((g * jax.nn.sigmoid(g) * u).astype(BF16), wd)


def _ffn_kernel(h_ref, wg_ref, wu_ref, wd_ref, x_ref, o_ref):
    j = pl.program_id(1)
    y = _swiglu_partial(h_ref[...], wg_ref[...], wu_ref[...], wd_ref[...])

    @pl.when(j == 0)
    def _():
        o_ref[...] = x_ref[...] + y

    @pl.when(j > 0)
    def _():
        o_ref[...] += y


def _dense_ffn(h, x, lw):
    n = x.shape[0]
    tm, tf = TM_FFN, TF_FFN
    return pl.pallas_call(
        _ffn_kernel,
        grid=(n // tm, D_FF // tf),
        in_specs=[pl.BlockSpec((tm, D_MODEL), lambda i, j: (i, 0)),
                  pl.BlockSpec((D_MODEL, tf), lambda i, j: (0, j)),
                  pl.BlockSpec((D_MODEL, tf), lambda i, j: (0, j)),
                  pl.BlockSpec((tf, D_MODEL), lambda i, j: (j, 0)),
                  pl.BlockSpec((tm, D_MODEL), lambda i, j: (i, 0))],
        out_specs=pl.BlockSpec((tm, D_MODEL), lambda i, j: (i, 0)),
        out_shape=jax.ShapeDtypeStruct((n, D_MODEL), F32),
        compiler_params=pltpu.CompilerParams(dimension_semantics=("parallel", "arbitrary"),
                                             vmem_limit_bytes=VMEM_LIMIT),
        name="dense_ffn",
    )(h, lw["w_gate"], lw["w_up"], lw["w_down"], x)


def _moe_kernel(blk_e_ref, n_used_ref, xs_ref, wg_ref, wu_ref, wd_ref, o_ref):
    i = pl.program_id(0)
    j = pl.program_id(1)
    used = i < n_used_ref[0]

    @pl.when(used)
    def _():
        y = _swiglu_partial(xs_ref[...], wg_ref[...], wu_ref[...], wd_ref[...])

        @pl.when(j == 0)
        def _():
            o_ref[...] = y

        @pl.when(j > 0)
        def _():
            o_ref[...] += y

    @pl.when(jnp.logical_not(used) & (j == 0))
    def _():
        o_ref[...] = jnp.zeros_like(o_ref)


def _moe_experts(xs, blk_e, n_used, lw):
    p = xs.shape[0]
    tm, tf = TM_MOE, TF_MOE
    n_ff = D_FF_EXPERT // tf

    def col(i, j):
        return jnp.where(i % 2 == 0, j, n_ff - 1 - j)

    grid_spec = pltpu.PrefetchScalarGridSpec(
        num_scalar_prefetch=2,
        grid=(p // tm, n_ff),
        in_specs=[pl.BlockSpec((tm, D_MODEL), lambda i, j, be, nu: (i, 0)),
                  pl.BlockSpec((None, D_MODEL, tf), lambda i, j, be, nu: (be[i], 0, col(i, j))),
                  pl.BlockSpec((None, D_MODEL, tf), lambda i, j, be, nu: (be[i], 0, col(i, j))),
                  pl.BlockSpec((None, tf, D_MODEL), lambda i, j, be, nu: (be[i], col(i, j), 0))],
        out_specs=pl.BlockSpec((tm, D_MODEL), lambda i, j, be, nu: (i, 0)),
    )
    return pl.pallas_call(
        _moe_kernel,
        grid_spec=grid_spec,
        out_shape=jax.ShapeDtypeStruct((p, D_MODEL), F32),
        compiler_params=pltpu.CompilerParams(dimension_semantics=("arbitrary", "arbitrary"),
                                             vmem_limit_bytes=VMEM_LIMIT),
        name="moe_experts",
    )(blk_e, n_used, xs, lw["w_gate"], lw["w_up"], lw["w_down"])


def _row_copy(src_ref, src_row, dst_ref, dst_row, sem):
    return pltpu.make_async_copy(src_ref.at[pl.ds(src_row, 1)], dst_ref.at[pl.ds(dst_row, 1)], sem)


def _dispatch_kernel(dest_ref, h_ref, xs_in_ref, xs_ref, sem, *, tm):
    del xs_in_ref

    def start(r, carry):
        for k in range(2):
            _row_copy(h_ref, r, xs_ref, dest_ref[0, 2 * r + k], sem).start()
        return carry

    def wait(r, carry):
        for k in range(2):
            _row_copy(h_ref, r, xs_ref, dest_ref[0, 2 * r + k], sem).wait()
        return carry

    lax.fori_loop(0, tm, start, 0, unroll=8)
    lax.fori_loop(0, tm, wait, 0, unroll=8)


def _dispatch(dest_blk, h, p):
    n = h.shape[0]
    tm = TM_ROUTE
    xs0 = jnp.zeros((p, D_MODEL), F32)
    return pl.pallas_call(
        functools.partial(_dispatch_kernel, tm=tm),
        grid=(n // tm,),
        in_specs=[pl.BlockSpec((None, 1, 2 * tm), lambda i: (i, 0, 0), memory_space=pltpu.SMEM),
                  pl.BlockSpec((tm, D_MODEL), lambda i: (i, 0)),
                  pl.BlockSpec(memory_space=pl.ANY)],
        out_specs=pl.BlockSpec(memory_space=pl.ANY),
        out_shape=jax.ShapeDtypeStruct((p, D_MODEL), F32),
        scratch_shapes=[pltpu.SemaphoreType.DMA(())],
        input_output_aliases={2: 0},
        compiler_params=pltpu.CompilerParams(dimension_semantics=("arbitrary",), vmem_limit_bytes=VMEM_LIMIT,
                                             has_side_effects=True),
        name="moe_dispatch",
    )(dest_blk, h, xs0)


def _combine_kernel(dest_ref, ys_ref, x_ref, sel_ref, o_ref, ybuf, sem, *, tm):
    def start(r, carry):
        for k in range(2):
            _row_copy(ys_ref, dest_ref[0, 2 * r + k], ybuf.at[k], r, sem).start()
        return carry

    def wait(r, carry):
        for k in range(2):
            _row_copy(ys_ref, dest_ref[0, 2 * r + k], ybuf.at[k], r, sem).wait()
        return carry

    lax.fori_loop(0, tm, start, 0, unroll=8)
    lax.fori_loop(0, tm, wait, 0, unroll=8)
    sel = sel_ref[...]
    o_ref[...] = x_ref[...] + ybuf[0] * sel[:, 2:3] + ybuf[1] * sel[:, 3:4]


def _combine(dest_blk, ys, x, sel):
    n = x.shape[0]
    tm = TM_ROUTE
    return pl.pallas_call(
        functools.partial(_combine_kernel, tm=tm),
        grid=(n // tm,),
        in_specs=[pl.BlockSpec((None, 1, 2 * tm), lambda i: (i, 0, 0), memory_space=pltpu.SMEM),
                  pl.BlockSpec(memory_space=pl.ANY),
                  pl.BlockSpec((tm, D_MODEL), lambda i: (i, 0)),
                  pl.BlockSpec((tm, LANES), lambda i: (i, 0))],
        out_specs=pl.BlockSpec((tm, D_MODEL), lambda i: (i, 0)),
        out_shape=jax.ShapeDtypeStruct((n, D_MODEL), F32),
        scratch_shapes=[pltpu.VMEM((2, tm, D_MODEL), F32), pltpu.SemaphoreType.DMA(())],
        compiler_params=pltpu.CompilerParams(dimension_semantics=("arbitrary",), vmem_limit_bytes=VMEM_LIMIT),
        name="moe_combine",
    )(dest_blk, ys, x, sel)


def _moe_ffn(h, x, sel, lw):
    n = x.shape[0]
    tm = TM_MOE
    top_idx = sel[:, :2].astype(jnp.int32)
    flat_e = top_idx.reshape(-1)
    onehot = (flat_e[:, None] == jnp.arange(N_EXPERTS, dtype=jnp.int32)[None, :]).astype(jnp.int32)
    csum = jnp.cumsum(onehot, axis=0)
    rank = jnp.sum(csum * onehot, axis=1) - 1
    counts = csum[-1]
    padded = (counts + tm - 1) // tm * tm
    pend = jnp.cumsum(padded)
    pstart = pend - padded
    dest = pstart[flat_e] + rank
    p = n * 2 + N_EXPERTS * tm
    n_blk = p // tm
    blk_start = jnp.arange(n_blk, dtype=jnp.int32) * tm
    blk_e = jnp.minimum(jnp.sum((pend[None, :] <= blk_start[:, None]).astype(jnp.int32), axis=1), N_EXPERTS - 1)
    n_used = (pend[-1:] // tm).astype(jnp.int32)
    dest_blk = dest.astype(jnp.int32).reshape(n // TM_ROUTE, 1, 2 * TM_ROUTE)
    xs = _dispatch(dest_blk, h, p)
    ys = _moe_experts(xs, blk_e, n_used, lw)
    return _combine(dest_blk, ys, x, sel)


def _pad_cols(w, width):
    return jnp.pad(w, ((0, 0), (0, width - w.shape[1])))


def _head_groups(w, heads, dim):
    kdim = w.shape[0]
    return jnp.pad(w.reshape(kdim, heads, dim), ((0, 0), (0, 0), (0, LANES - dim))).reshape(kdim, heads * LANES)


def _layer_weights(layer, p):
    w_in = p["w_in"][layer]
    a, b, c = w_in[:, :COLS_A], w_in[:, COLS_A:COLS_A + COLS_B], w_in[:, COLS_A + COLS_B:]
    kpe = jnp.pad(a[:, MLA_Q_RANK + MLA_KV_RANK:], ((0, 0), (MLA_NOPE, LANES - MLA_QK)))
    bq = b[:, :WIDTH_B]

    def dup(w):
        return jnp.repeat(w.reshape(D_MODEL, SWA_KV_HEADS, 1, HEAD_DIM), 2, axis=2).reshape(D_MODEL, 2 * LANES)

    bk = dup(b[:, WIDTH_B:WIDTH_B + SWA_KV_HEADS * HEAD_DIM])
    bv = dup(b[:, WIDTH_B + SWA_KV_HEADS * HEAD_DIM:])
    w_in_p = jnp.concatenate([a[:, :MLA_Q_RANK + MLA_KV_RANK], kpe, bq, bk, bv, c], axis=1).astype(BF16)
    w_ukv = p["mla_w_ukv"][layer].reshape(MLA_KV_RANK, MLA_HEADS, MLA_NOPE + MLA_V)
    w_uk = _head_groups(w_ukv[:, :, :MLA_NOPE].reshape(MLA_KV_RANK, -1), MLA_HEADS, MLA_NOPE)
    w_uv = w_ukv[:, :, MLA_NOPE:].reshape(MLA_KV_RANK, -1)
    row = lambda g: g.reshape(1, -1).astype(F32)
    lw = {
        "attn_g": row(p["attn_norm_g"][layer]),
        "w_in": w_in_p,
        "cq_g": row(p["mla_cq_norm_g"][layer]),
        "w_uq": _head_groups(p["mla_w_uq"][layer], MLA_HEADS, MLA_QK).astype(BF16),
        "ckv_g": row(p["mla_ckv_norm_g"][layer]),
        "w_ukv": jnp.concatenate([w_uk, w_uv], axis=1).astype(BF16),
        "mla_qg": _pad_cols(row(p["mla_qn_g"][layer]), LANES),
        "mla_kg": _pad_cols(row(p["mla_kn_g"][layer]), LANES),
        "swa_qg": jnp.tile(row(p["swa_qn_g"][layer]), (1, 2)),
        "swa_kg": jnp.tile(row(p["swa_kn_g"][layer]), (1, 2)),
        "sinks": p["swa_sinks"][layer].astype(F32),
        "out_g": row(p["out_norm_g"][layer]),
        "w_out": p["w_out"][layer].astype(BF16),
        "ffn_g": row(p["ffn_norm_g"][layer]),
    }
    j = layer // 2
    if layer % 2 == 0:
        lw["w_gate"] = p["dense_w_gate"][j].astype(BF16)
        lw["w_up"] = p["dense_w_up"][j].astype(BF16)
        lw["w_down"] = p["dense_w_down"][j].astype(BF16)
    else:
        rw = _pad_cols(p["router_w"][j], LANES)
        hi = rw.astype(BF16)
        lw["router_hi"] = hi
        lw["router_lo"] = (rw - hi.astype(F32)).astype(BF16)
        lw["w_gate"] = p["moe_w_gate"][j].astype(BF16)
        lw["w_up"] = p["moe_w_up"][j].astype(BF16)
        lw["w_down"] = p["moe_w_down"][j].astype(BF16)
    return lw


def _rope_tables(seq):
    half = MLA_ROPE // 2
    inv = ROPE_THETA ** (-jnp.arange(half, dtype=F32) / half)
    ang = jnp.arange(seq).astype(F32)[:, None] * inv[None, :]
    cos, sin = jnp.cos(ang), jnp.sin(ang)
    zl = jnp.zeros((seq, MLA_NOPE), F32)
    zh = jnp.zeros((seq, half), F32)
    zr = jnp.zeros((seq, LANES - MLA_QK), F32)
    cos_t = jnp.concatenate([zl + 1.0, cos, cos, zr + 1.0], axis=1)
    sina = jnp.concatenate([zl, -sin, zh, zr], axis=1)
    sinb = jnp.concatenate([zl, zh, sin, zr], axis=1)
    return cos_t, sina, sinb


def kernel(x, attn_norm_g, w_in, mla_cq_norm_g, mla_w_uq, mla_ckv_norm_g, mla_w_ukv, mla_qn_g, mla_kn_g, swa_qn_g, swa_kn_g, swa_sinks, out_norm_g, w_out, ffn_norm_g, dense_w_gate, dense_w_up, dense_w_down, router_w, moe_w_gate, moe_w_up, moe_w_down):
    params = dict(attn_norm_g=attn_norm_g, w_in=w_in, mla_cq_norm_g=mla_cq_norm_g, mla_w_uq=mla_w_uq,
                  mla_ckv_norm_g=mla_ckv_norm_g, mla_w_ukv=mla_w_ukv, mla_qn_g=mla_qn_g, mla_kn_g=mla_kn_g,
                  swa_qn_g=swa_qn_g, swa_kn_g=swa_kn_g, swa_sinks=swa_sinks, out_norm_g=out_norm_g, w_out=w_out,
                  ffn_norm_g=ffn_norm_g, dense_w_gate=dense_w_gate, dense_w_up=dense_w_up,
                  dense_w_down=dense_w_down, router_w=router_w, moe_w_gate=moe_w_gate, moe_w_up=moe_w_up,
                  moe_w_down=moe_w_down)
    batch, seq, _ = x.shape
    depth = w_in.shape[0]
    tabs = _rope_tables(seq)
    slopes = jnp.exp2(-8.0 * jnp.arange(1, SWA_Q_HEADS + 1, dtype=F32) / SWA_Q_HEADS)
    tri = (jnp.arange(TC_SB)[:, None] > jnp.arange(TC_SB)[None, :]).astype(BF16)
    xf =x.reshape(batch * seq, D_MODEL)
    for layer in range(depth):
        lw = _layer_weights(layer, params)
        mq, mk, mv, sq, sk, sv, cq, ck, cv = _in_proj(xf, lw, tabs, seq)
        o_a = _mla_attention(mq, mk, mv, batch, seq)
        o_b = _swa_attention(sq, sk, sv, slopes, lw["sinks"], batch, seq)
        o_c = _sb_attention(cq, ck, cv, tri, batch, seq)
        if layer % 2 == 0:
            xf, h = _out_proj(o_a, o_b, o_c, xf, lw, moe=False)
            xf = _dense_ffn(h, xf, lw)
        else:
            xf, h, sel = _out_proj(o_a, o_b, o_c, xf, lw, moe=True)
            xf = _moe_ffn(h, xf, sel, lw)
    return xf.reshape(batch, seq, D_MODEL)
```

```python
import functools

import jax
import jax.numpy as jnp
from jax import lax
from jax.experimental import pallas as pl
from jax.experimental.pallas import tpu as pltpu

F32 = jnp.float32
BF16 = jnp.bfloat16

D_MODEL = 1024
HEAD_DIM = 64
MLA_HEADS = 4
MLA_Q_RANK = 256
MLA_KV_RANK = 128
MLA_NOPE = 64
MLA_ROPE = 32
MLA_QK = MLA_NOPE + MLA_ROPE
MLA_V = 64
ROPE_THETA = 10000.0
SWA_Q_HEADS = 8
SWA_KV_HEADS = 2
SWA_WINDOW = 128
SB_HEADS = 4
WIDTH_A = MLA_HEADS * MLA_V
WIDTH_B = SWA_Q_HEADS * HEAD_DIM
WIDTH_C = SB_HEADS * HEAD_DIM
COLS_A = MLA_Q_RANK + MLA_KV_RANK + MLA_ROPE
COLS_B = (SWA_Q_HEADS + 2 * SWA_KV_HEADS) * HEAD_DIM
D_FF = 2816
N_EXPERTS = 8
D_FF_EXPERT = 3584
RMS_EPS = 1e-6
NEG = -1e30
LOG2E = 1.4426950408889634

LANES = 128
VMEM_LIMIT = 56 * 1024 * 1024

TM_PROJ = 512
TM_IN = 1024
TQ_ATTN = 256
TK_ATTN = 1024
TK_SB = 256
TC_SB = 256
SB_DEAD_BITS = 151.0
TQ_SWA = 512
TM_FFN = 256
TF_FFN = 2816
TM_MOE = 512
TM_ROUTE = 512
TF_MOE = 1792

P_CQ = 0
P_CKV = 256
P_KPE = 384
P_SWQ = 512
P_SWK = 1024
P_SWV = 1280
P_SBQ = 1536
P_SBK = 1792
P_SBV = 2048
P_COLS = 2304


def _dot(a, b):
    return jnp.dot(a, b, preferred_element_type=F32)


def _dot_nt(a, b):
    return lax.dot_general(a, b, (((1,), (1,)), ((), ())), preferred_element_type=F32)


def _rms(x, g):
    return x * lax.rsqrt(jnp.mean(x * x, axis=-1, keepdims=True) + RMS_EPS) * g


def _in_proj_kernel(x_ref, g_ref, w_ref, gcq_ref, wuq_ref, gckv_ref, wukv_ref, gq_ref, gk_ref,
                    cos_ref, sina_ref, sinb_ref, sgq_ref, sgk_ref,
                    mq_ref, mk_ref, mv_ref, sq_ref, sk_ref, sv_ref, cq_ref, ck_ref, cv_ref, *, sub):
    lane = lax.broadcasted_iota(jnp.int32, (1, LANES), 1)
    low = lane < HEAD_DIM
    scale = HEAD_DIM ** -0.5

    def head_norm(t, g):
        return t * lax.rsqrt(jnp.sum(t * t, axis=-1, keepdims=True) * (1.0 / MLA_QK) + RMS_EPS) * g

    def pair_norm(t, g):
        t2 = t * t
        lo = jnp.sum(jnp.where(low, t2, 0.0), axis=-1, keepdims=True)
        hi = jnp.sum(jnp.where(low, 0.0, t2), axis=-1, keepdims=True)
        ms = jnp.where(low, lo, hi) * (1.0 / HEAD_DIM)
        return t * lax.rsqrt(ms + RMS_EPS) * g

    def rows(rs):
        h = _rms(x_ref[rs, :], g_ref[...]).astype(BF16)

        def proj(lo, hi):
            return _dot(h, w_ref[:, lo:hi])

        cos = cos_ref[rs, :]
        sina = sina_ref[rs, :]
        sinb = sinb_ref[rs, :]

        def rope(t):
            return t * cos + pltpu.roll(t, LANES - MLA_ROPE // 2, 1) * sina + pltpu.roll(t, MLA_ROPE // 2, 1) * sinb

        a = proj(P_CQ, P_SWQ)
        q = _dot(_rms(a[:, :MLA_Q_RANK], gcq_ref[...]).astype(BF16), wuq_ref[...])
        for hh in range(MLA_HEADS):
            sl = slice(LANES * hh, LANES * (hh + 1))
            mq_ref[rs, sl] = (rope(head_norm(q[:, sl], gq_ref[...])) * (MLA_QK ** -0.5 * LOG2E)).astype(BF16)
        kv = _dot(_rms(a[:, P_CKV:P_KPE], gckv_ref[...]).astype(BF16), wukv_ref[...])
        kpe = a[:, P_KPE:P_SWQ]
        for hh in range(MLA_HEADS):
            sl = slice(LANES * hh, LANES * (hh + 1))
            mk_ref[rs, sl] = rope(head_norm(kv[:, sl] + kpe, gk_ref[...])).astype(BF16)
        mv_ref[rs, :] = kv[:, MLA_HEADS * LANES:].astype(BF16)
        bq = proj(P_SWQ, P_SWK)
        for gi in range(WIDTH_B // LANES):
            sl = slice(LANES * gi, LANES * (gi + 1))
            sq_ref[rs, sl] = (pair_norm(bq[:, sl], sgq_ref[...]) * (scale * LOG2E)).astype(BF16)
        bk = proj(P_SWK, P_SWV)
        for gi in range(SWA_KV_HEADS):
            sl = slice(LANES * gi, LANES * (gi + 1))
            sk_ref[rs, sl] = pair_norm(bk[:, sl], sgk_ref[...]).astype(BF16)
        sv_ref[rs, :] = proj(P_SWV, P_SBQ).astype(BF16)
        cq_ref[rs, :] = (proj(P_SBQ, P_SBK) * (scale * LOG2E)).astype(BF16)
        ck_ref[rs, :] = proj(P_SBK, P_SBV).astype(BF16)
        cv_ref[rs, :] = proj(P_SBV, P_COLS).astype(BF16)

    for r0 in range(0, x_ref.shape[0], sub):
        rows(slice(r0, r0 + sub))


def _in_proj(x, lw, tabs, seq):
    n = x.shape[0]
    tm = TM_IN
    nt = seq // tm
    row = lambda i: (i, 0)
    fixed = lambda i: (0, 0)
    pos = lambda i: (i % nt, 0)
    full = lambda a: pl.BlockSpec(a.shape, fixed)
    cos, sina, sinb = tabs
    widths = (4 * LANES, 4 * LANES, WIDTH_A, WIDTH_B, 2 * LANES, 2 * LANES, WIDTH_C, WIDTH_C, WIDTH_C)
    return pl.pallas_call(
        functools.partial(_in_proj_kernel, sub=TM_PROJ),
        grid=(n // tm,),
        in_specs=[pl.BlockSpec((tm, D_MODEL), row), full(lw["attn_g"]), full(lw["w_in"]), full(lw["cq_g"]),
                  full(lw["w_uq"]), full(lw["ckv_g"]), full(lw["w_ukv"]), full(lw["mla_qg"]), full(lw["mla_kg"]),
                  pl.BlockSpec((tm, LANES), pos), pl.BlockSpec((tm, LANES), pos), pl.BlockSpec((tm, LANES), pos),
                  full(lw["swa_qg"]), full(lw["swa_kg"])],
        out_specs=[pl.BlockSpec((tm, w), row) for w in widths],
        out_shape=[jax.ShapeDtypeStruct((n, w), BF16) for w in widths],
        compiler_params=pltpu.CompilerParams(dimension_semantics=("parallel",), vmem_limit_bytes=VMEM_LIMIT),
        name="in_proj",
    )(x, lw["attn_g"], lw["w_in"], lw["cq_g"], lw["w_uq"], lw["ckv_g"], lw["w_ukv"], lw["mla_qg"], lw["mla_kg"],
      cos, sina, sinb, lw["swa_qg"], lw["swa_kg"])


def _mla_kernel(q_ref, k_ref, v_ref, o_ref, sa_ref, sb_ref, m_ref, l_ref, acc_ref, *, tq, tk):
    i = pl.program_id(2)
    n_full = (i * tq) // tk
    m_ref[...] = jnp.full(m_ref.shape, NEG, F32)
    l_ref[...] = jnp.zeros(l_ref.shape, F32)
    acc_ref[...] = jnp.zeros(acc_ref.shape, F32)

    def scores(j, s_ref):
        start = pl.multiple_of(j * tk, tk)
        k = k_ref[pl.ds(start, tk), :]
        for hh in range(2):
            sl = slice(LANES * hh, LANES * (hh + 1))
            s_ref[hh] = _dot_nt(q_ref[:, sl], k[:, sl])

    def absorb(j, s_ref, masked, width=tk):
        start = pl.multiple_of(j * tk, tk)
        v = v_ref[pl.ds(start, width), :]
        if masked:
            diff = lax.broadcasted_iota(jnp.int32, (tq, width), 1) - lax.broadcasted_iota(jnp.int32, (tq, width), 0)
            keep = diff <= i * tq - start
        for hh in range(2):
            s = s_ref[hh, :, :width]
            if masked:
                s = jnp.where(keep, s, NEG)
            m = m_ref[hh]
            m_new = jnp.maximum(m, jnp.max(s, axis=-1, keepdims=True))
            alpha = jnp.exp2(m - m_new)
            p = jnp.exp2(s - m_new)
            l_ref[hh] = alpha * l_ref[hh] + jnp.sum(p, axis=-1, keepdims=True)
            acc_ref[hh] = alpha * acc_ref[hh] + _dot(p.astype(BF16), v)
            m_ref[hh] = m_new

    scores(0, sa_ref)

    def pair(t, carry):
        scores(2 * t + 1, sb_ref)
        absorb(2 * t, sa_ref, False)
        scores(2 * t + 2, sa_ref)
        absorb(2 * t + 1, sb_ref, False)
        return carry

    lax.fori_loop(0, n_full // 2, pair, 0)
    base = 2 * (n_full // 2)

    def absorb_diagonal(s_ref):
        blocks = (i * tq - n_full * tk) // tq + 1
        for nb in range(1, tk // tq + 1):
            pl.when(blocks == nb)(functools.partial(absorb, n_full, s_ref, True, nb * tq))

    @pl.when(n_full % 2 == 1)
    def _():
        scores(base + 1, sb_ref)
        absorb(base, sa_ref, False)
        absorb_diagonal(sb_ref)

    @pl.when(n_full % 2 == 0)
    def _():
        absorb_diagonal(sa_ref)

    lane = lax.broadcasted_iota(jnp.int32, (1, LANES), 1)
    o_ref[...] = jnp.where(lane < MLA_V, acc_ref[0] / l_ref[0], acc_ref[1] / l_ref[1])


def _mla_attention(q, k, v, batch, seq):
    tq, tk = TQ_ATTN, TK_ATTN
    q = q.reshape(batch, seq, -1)
    k = k.reshape(batch, seq, -1)
    v = v.reshape(batch, seq, -1)
    out = pl.pallas_call(
        functools.partial(_mla_kernel, tq=tq, tk=tk),
        grid=(batch, MLA_HEADS // 2, seq // tq),
        in_specs=[pl.BlockSpec((None, tq, 2 * LANES), lambda b, p, i: (b, i, p)),
                  pl.BlockSpec((None, seq, 2 * LANES), lambda b, p, i: (b, 0, p)),
                  pl.BlockSpec((None, seq, LANES), lambda b, p, i: (b, 0, p))],
        out_specs=pl.BlockSpec((None, tq, LANES), lambda b, p, i: (b, i, p)),
        out_shape=jax.ShapeDtypeStruct((batch, seq, WIDTH_A), F32),
        scratch_shapes=[pltpu.VMEM((2, tq, tk), F32), pltpu.VMEM((2, tq, tk), F32),
                        pltpu.VMEM((2, tq, 1), F32), pltpu.VMEM((2, tq, 1), F32), pltpu.VMEM((2, tq, LANES), F32)],
        compiler_params=pltpu.CompilerParams(dimension_semantics=("parallel", "parallel", "arbitrary"),
                                             vmem_limit_bytes=VMEM_LIMIT),
        name="mla_attention",
    )(q, k, v)
    return out.reshape(batch * seq, WIDTH_A)


def _sb_kernel(q_ref, k_ref, v_ref, tri_ref, o_ref, za_ref, zb_ref, c_ref, acc_ref, *, tq, tk, tc):
    i = pl.program_id(2)
    n_full = (i * tq) // tk
    lane = lax.broadcasted_iota(jnp.int32, (1, LANES), 1)
    low = lane < HEAD_DIM
    c_ref[...] = jnp.zeros(c_ref.shape, F32)
    acc_ref[...] = jnp.zeros(acc_ref.shape, F32)
    sign = jnp.uint32(0x80000000)

    def logits(j, z_ref):
        start = pl.multiple_of(j * tk, tk)
        k = k_ref[pl.ds(start, tk), :]
        q = q_ref[...]
        zero = jnp.zeros_like(q)
        z_ref[0] = _dot_nt(jnp.where(low, q, zero), k)
        z_ref[1] = _dot_nt(jnp.where(low, zero, q), k)

    def absorb(j, z_ref, masked, width=tk):
        start = pl.multiple_of(j * tk, tk)
        v = v_ref[pl.ds(start, width), :]
        tri = tri_ref[...]
        if masked:
            diff = lax.broadcasted_iota(jnp.int32, (tq, width), 1) - lax.broadcasted_iota(jnp.int32, (tq, width), 0)
            keep = diff < i * tq - start
        staged = []
        for hh in range(2):
            z = z_ref[hh, :, :width]
            neg_abs = lax.bitcast_convert_type(lax.bitcast_convert_type(z, jnp.uint32) | sign, F32)
            sp = jnp.maximum(z, 0.0) + jnp.log2(1.0 + jnp.exp2(neg_abs))
            if masked:
                sp = jnp.where(keep, sp, 0.0)
            c = c_ref[hh]
            parts = []
            for cc in reversed(range(width // tc)):
                cs = slice(cc * tc, (cc + 1) * tc)
                spc = sp[:, cs]
                right = _dot(spc.astype(BF16), tri)
                parts.append((cs, z[:, cs] - spc, right, c))
                c = c + right[:, 0:1] + spc[:, 0:1]
            c_ref[hh] = c
            staged.append(parts)
        for hh in range(2):
            acc = acc_ref[hh]
            for cs, log_beta, right, c in staged[hh]:
                a = jnp.exp2(log_beta - right - c)
                if masked:
                    a = jnp.where(keep[:, cs], a, 0.0)
                acc = acc + _dot(a.astype(BF16), v[cs, :])
            acc_ref[hh] = acc

    def live():
        return jnp.min(c_ref[...]) < SB_DEAD_BITS

    def pairs(first_ref, second_ref):
        npair = n_full // 2

        def cond(carry):
            t, alive = carry
            return jnp.logical_and(t < npair, alive)

        def body(carry):
            t, _ = carry
            top = 2 * (npair - t) - 1
            logits(top - 1, second_ref)
            absorb(top, first_ref, False)

            @pl.when(live())
            def _():
                logits(jnp.maximum(top - 2, 0), first_ref)
                absorb(top - 1, second_ref, False)

            return t + 1, live()

        lax.while_loop(cond, body, (jnp.int32(0), live()))

    def absorb_diagonal(z_ref, next_ref):
        blocks = (i * tq - n_full * tk) // tq + 1
        for nb in range(1, tk // tq + 1):
            @pl.when(blocks == nb)
            def _():
                logits(jnp.maximum(n_full - 1, 0), next_ref)
                absorb(n_full, z_ref, True, nb * tq)

    logits(n_full, za_ref)
    absorb_diagonal(za_ref, zb_ref)

    @pl.when(n_full % 2 == 1)
    def _():
        @pl.when(live())
        def _():
            logits(jnp.maximum(n_full - 2, 0), za_ref)
            absorb(n_full - 1, zb_ref, False)

        pairs(za_ref, zb_ref)

    @pl.when(n_full % 2 == 0)
    def _():
        pairs(zb_ref, za_ref)

    o_ref[...] = jnp.where(low, acc_ref[0], acc_ref[1])


def _sb_attention(q, k, v, tri, batch, seq):
    tq, tk, tc = TQ_ATTN, TK_SB, TC_SB
    q = q.reshape(batch, seq, -1)
    k = k.reshape(batch, seq, -1)
    v = v.reshape(batch, seq, -1)
    out = pl.pallas_call(
        functools.partial(_sb_kernel, tq=tq, tk=tk, tc=tc),
        grid=(batch, SB_HEADS // 2, seq // tq),
        in_specs=[pl.BlockSpec((None, tq, LANES), lambda b, p, i: (b, i, p)),
                  pl.BlockSpec((None, seq, LANES), lambda b, p, i: (b, 0, p)),
                  pl.BlockSpec((None, seq, LANES), lambda b, p, i: (b, 0, p)),
                  pl.BlockSpec((tc, tc), lambda b, p, i: (0, 0))],
        out_specs=pl.BlockSpec((None, tq, LANES), lambda b, p, i: (b, i, p)),
        out_shape=jax.ShapeDtypeStruct((batch, seq, WIDTH_C), F32),
        scratch_shapes=[pltpu.VMEM((2, tq, tk), F32), pltpu.VMEM((2, tq, tk), F32),
                        pltpu.VMEM((2, tq, 1), F32), pltpu.VMEM((2, tq, LANES), F32)],
        compiler_params=pltpu.CompilerParams(dimension_semantics=("parallel", "parallel", "arbitrary"),
                                             vmem_limit_bytes=VMEM_LIMIT),
        name="sb_attention",
    )(q, k, v, tri)
    return out.reshape(batch * seq, WIDTH_C)


def _swa_kernel(slopes_ref, sinks_ref, q_ref, k_ref, v_ref, o_ref, *, tq):
    w = SWA_WINDOW
    group = SWA_Q_HEADS // SWA_KV_HEADS
    hkv = pl.program_id(1)
    i = pl.program_id(2)
    lane = lax.broadcasted_iota(jnp.int32, (1, LANES), 1)
    low = lane < HEAD_DIM
    skew = (lax.broadcasted_iota(jnp.int32, (w, 2 * w), 0) - lax.broadcasted_iota(jnp.int32, (w, 2 * w), 1))

    def biases(offset):
        dist = offset + skew
        valid = (dist >= 0) & (dist < w)
        distf = dist.astype(F32)
        return [jnp.where(valid, -(slopes_ref[hkv * group + g] * LOG2E) * distf, NEG) for g in range(group)]

    regular = biases(w)
    for n in range(tq // w):
        q0 = i * tq + n * w
        ks = pl.multiple_of(jnp.maximum(q0 - w, 0), w)
        k = k_ref[pl.ds(ks, 2 * w), :]
        v = v_ref[pl.ds(ks, 2 * w), :]
        bias = biases(q0 - ks) if n == 0 else regular
        for pr in range(group // 2):
            qp = q_ref[n * w:(n + 1) * w, LANES * pr:LANES * (pr + 1)]
            zero = jnp.zeros_like(qp)
            outs = []
            for half in range(2):
                head = hkv * group + pr * 2 + half
                qh = jnp.where(low, qp, zero) if half == 0 else jnp.where(low, zero, qp)
                s = _dot_nt(qh, k) + bias[pr * 2 + half]
                sink = sinks_ref[head] * LOG2E
                m = jnp.maximum(jnp.max(s, axis=-1, keepdims=True), sink)
                p = jnp.exp2(s - m)
                den = jnp.sum(p, axis=-1, keepdims=True) + jnp.exp2(sink - m)
                outs.append(_dot(p.astype(BF16), v) / den)
            o_ref[n * w:(n + 1) * w, LANES * pr:LANES * (pr + 1)] = jnp.where(low, outs[0], outs[1])


def _swa_attention(q, k, v, slopes, sinks, batch, seq):
    tq = TQ_SWA
    q = q.reshape(batch, seq, -1)
    k = k.reshape(batch, seq, -1)
    v = v.reshape(batch, seq, -1)
    smem = pl.BlockSpec(memory_space=pltpu.SMEM)
    out = pl.pallas_call(
        functools.partial(_swa_kernel, tq=tq),
        grid=(batch, SWA_KV_HEADS, seq // tq),
        in_specs=[smem, smem,
                  pl.BlockSpec((None, tq, 2 * LANES), lambda b, h, i: (b, i, h)),
                  pl.BlockSpec((None, seq, LANES), lambda b, h, i: (b, 0, h)),
                  pl.BlockSpec((None, seq, LANES), lambda b, h, i: (b, 0, h))],
        out_specs=pl.BlockSpec((None, tq, 2 * LANES), lambda b, h, i: (b, i, h)),
        out_shape=jax.ShapeDtypeStruct((batch, seq, WIDTH_B), F32),
        compiler_params=pltpu.CompilerParams(dimension_semantics=("parallel", "parallel", "arbitrary"),
                                             vmem_limit_bytes=VMEM_LIMIT),
        name="swa_attention",
    )(slopes, sinks, q, k, v)
    return out.reshape(batch * seq, WIDTH_B)


def _out_proj_kernel(*refs, moe):
    if moe:
        oa, ob, oc, x_ref, gout, wout, gffn, wr_hi, wr_lo, xo_ref, h_ref, sel_ref = refs
    else:
        oa, ob, oc, x_ref, gout, wout, gffn, xo_ref, h_ref = refs
    g = gout[...]
    b0, b1 = WIDTH_A, WIDTH_A + WIDTH_B
    na = _rms(oa[...], g[:, :b0]).astype(BF16)
    nb = _rms(ob[...], g[:, b0:b1]).astype(BF16)
    nc = _rms(oc[...], g[:, b1:]).astype(BF16)
    xn = x_ref[...] + _dot(na, wout[:b0, :]) + _dot(nb, wout[b0:b1, :]) + _dot(nc, wout[b1:, :])
    xo_ref[...] = xn
    hf = _rms(xn, gffn[...])
    hb = hf.astype(BF16)
    h_ref[...] = hf if moe else hb
    if moe:
        lo = (hf - hb.astype(F32)).astype(BF16)
        logits = _dot(hb, wr_hi[...]) + _dot(lo, wr_hi[...]) + _dot(hb, wr_lo[...])
        lane = lax.broadcasted_iota(jnp.int32, logits.shape, 1)
        lg = jnp.where(lane < N_EXPERTS, logits, -jnp.inf)
        m1 = jnp.max(lg, axis=-1, keepdims=True)
        i1 = jnp.min(jnp.where(lg == m1, lane, LANES), axis=-1, keepdims=True)
        lg2 = jnp.where(lane == i1, -jnp.inf, lg)
        m2 = jnp.max(lg2, axis=-1, keepdims=True)
        i2 = jnp.min(jnp.where(lg2 == m2, lane, LANES), axis=-1, keepdims=True)
        e = jnp.exp(m2 - m1)
        g1 = 1.0 / (1.0 + e)
        g2 = e / (1.0 + e)
        sel_ref[...] = jnp.where(lane == 0, i1.astype(F32),
                                 jnp.where(lane == 1, i2.astype(F32),
                                           jnp.where(lane == 2, g1, jnp.where(lane == 3, g2, 0.0))))


def _out_proj(oa, ob, oc, x, lw, moe):
    n = x.shape[0]
    tm = TM_PROJ
    row = lambda i: (i, 0)
    fixed = lambda i: (0, 0)
    full = lambda a: pl.BlockSpec(a.shape, fixed)
    args = [oa, ob, oc, x, lw["out_g"], lw["w_out"], lw["ffn_g"]]
    in_specs = [pl.BlockSpec((tm, WIDTH_A), row), pl.BlockSpec((tm, WIDTH_B), row), pl.BlockSpec((tm, WIDTH_C), row),
                pl.BlockSpec((tm, D_MODEL), row), full(lw["out_g"]), full(lw["w_out"]), full(lw["ffn_g"])]
    out_specs = [pl.BlockSpec((tm, D_MODEL), row), pl.BlockSpec((tm, D_MODEL), row)]
    out_shape = [jax.ShapeDtypeStruct((n, D_MODEL), F32), jax.ShapeDtypeStruct((n, D_MODEL), F32 if moe else BF16)]
    if moe:
        args += [lw["router_hi"], lw["router_lo"]]
        in_specs += [full(lw["router_hi"]), full(lw["router_lo"])]
        out_specs.append(pl.BlockSpec((tm, LANES), row))
        out_shape.append(jax.ShapeDtypeStruct((n, LANES), F32))
    return pl.pallas_call(
        functools.partial(_out_proj_kernel, moe=moe),
        grid=(n // tm,),
        in_specs=in_specs, out_specs=out_specs, out_shape=out_shape,
        compiler_params=pltpu.CompilerParams(dimension_semantics=("parallel",), vmem_limit_bytes=VMEM_LIMIT),
        name="out_proj_moe" if moe else "out_proj",
    )(*args)


def _swiglu_partial(h, wg, wu, wd):
    h = h.astype(BF16)
    g = _dot(h, wg)
    u = _dot(h, wu)
    return _dot((g * jax.nn.sigmoid(g) * u).astype(BF16), wd)


def _ffn_kernel(h_ref, wg_ref, wu_ref, wd_ref, x_ref, o_ref):
    j = pl.program_id(1)
    y = _swiglu_partial(h_ref[...], wg_ref[...], wu_ref[...], wd_ref[...])

    @pl.when(j == 0)
    def _():
        o_ref[...] = x_ref[...] + y

    @pl.when(j > 0)
    def _():
        o_ref[...] += y


def _dense_ffn(h, x, lw):
    n = x.shape[0]
    tm, tf = TM_FFN, TF_FFN
    return pl.pallas_call(
        _ffn_kernel,
        grid=(n // tm, D_FF // tf),
        in_specs=[pl.BlockSpec((tm, D_MODEL), lambda i, j: (i, 0)),
                  pl.BlockSpec((D_MODEL, tf), lambda i, j: (0, j)),
                  pl.BlockSpec((D_MODEL, tf), lambda i, j: (0, j)),
                  pl.BlockSpec((tf, D_MODEL), lambda i, j: (j, 0)),
                  pl.BlockSpec((tm, D_MODEL), lambda i, j: (i, 0))],
        out_specs=pl.BlockSpec((tm, D_MODEL), lambda i, j: (i, 0)),
        out_shape=jax.ShapeDtypeStruct((n, D_MODEL), F32),
        compiler_params=pltpu.CompilerParams(dimension_semantics=("parallel", "arbitrary"),
                                             vmem_limit_bytes=VMEM_LIMIT),
        name="dense_ffn",
    )(h, lw["w_gate"], lw["w_up"], lw["w_down"], x)


def _moe_kernel(blk_e_ref, n_used_ref, xs_ref, wg_ref, wu_ref, wd_ref, o_ref):
    i = pl.program_id(0)
    j = pl.program_id(1)
    used = i < n_used_ref[0]

    @pl.when(used)
    def _():
        y = _swiglu_partial(xs_ref[...], wg_ref[...], wu_ref[...], wd_ref[...])

        @pl.when(j == 0)
        def _():
            o_ref[...] = y

        @pl.when(j > 0)
        def _():
            o_ref[...] += y

    @pl.when(jnp.logical_not(used) & (j == 0))
    def _():
        o_ref[...] = jnp.zeros_like(o_ref)


def _moe_experts(xs, blk_e, n_used, lw):
    p = xs.shape[0]
    tm, tf = TM_MOE, TF_MOE
    n_ff = D_FF_EXPERT // tf

    def col(i, j):
        return jnp.where(i % 2 == 0, j, n_ff - 1 - j)

    grid_spec = pltpu.PrefetchScalarGridSpec(
        num_scalar_prefetch=2,
        grid=(p // tm, n_ff),
        in_specs=[pl.BlockSpec((tm, D_MODEL), lambda i, j, be, nu: (i, 0)),
                  pl.BlockSpec((None, D_MODEL, tf), lambda i, j, be, nu: (be[i], 0, col(i, j))),
                  pl.BlockSpec((None, D_MODEL, tf), lambda i, j, be, nu: (be[i], 0, col(i, j))),
                  pl.BlockSpec((None, tf, D_MODEL), lambda i, j, be, nu: (be[i], col(i, j), 0))],
        out_specs=pl.BlockSpec((tm, D_MODEL), lambda i, j, be, nu: (i, 0)),
    )
    return pl.pallas_call(
        _moe_kernel,
        grid_spec=grid_spec,
        out_shape=jax.ShapeDtypeStruct((p, D_MODEL), F32),
        compiler_params=pltpu.CompilerParams(dimension_semantics=("arbitrary", "arbitrary"),
                                             vmem_limit_bytes=VMEM_LIMIT),
        name="moe_experts",
    )(blk_e, n_used, xs, lw["w_gate"], lw["w_up"], lw["w_down"])


def _row_copy(src_ref, src_row, dst_ref, dst_row, sem):
    return pltpu.make_async_copy(src_ref.at[pl.ds(src_row, 1)], dst_ref.at[pl.ds(dst_row, 1)], sem)


def _dispatch_kernel(dest_ref, h_ref, xs_in_ref, xs_ref, sem, *, tm):
    del xs_in_ref

    def start(r, carry):
        for k in range(2):
            _row_copy(h_ref, r, xs_ref, dest_ref[0, 2 * r + k], sem).start()
        return carry

    def wait(r, carry):
        for k in range(2):
            _row_copy(h_ref, r, xs_ref, dest_ref[0, 2 * r + k], sem).wait()
        return carry

    lax.fori_loop(0, tm, start, 0, unroll=8)
    lax.fori_loop(0, tm, wait, 0, unroll=8)


def _dispatch(dest_blk, h, p):
    n = h.shape[0]
    tm = TM_ROUTE
    xs0 = jnp.zeros((p, D_MODEL), F32)
    return pl.pallas_call(
        functools.partial(_dispatch_kernel, tm=tm),
        grid=(n // tm,),
        in_specs=[pl.BlockSpec((None, 1, 2 * tm), lambda i: (i, 0, 0), memory_space=pltpu.SMEM),
                  pl.BlockSpec((tm, D_MODEL), lambda i: (i, 0)),
                  pl.BlockSpec(memory_space=pl.ANY)],
        out_specs=pl.BlockSpec(memory_space=pl.ANY),
        out_shape=jax.ShapeDtypeStruct((p, D_MODEL), F32),
        scratch_shapes=[pltpu.SemaphoreType.DMA(())],
        input_output_aliases={2: 0},
        compiler_params=pltpu.CompilerParams(dimension_semantics=("arbitrary",), vmem_limit_bytes=VMEM_LIMIT,
                                             has_side_effects=True),
        name="moe_dispatch",
    )(dest_blk, h, xs0)


def _combine_kernel(dest_ref, ys_ref, x_ref, sel_ref, o_ref, ybuf, sem, *, tm):
    def start(r, carry):
        for k in range(2):
            _row_copy(ys_ref, dest_ref[0, 2 * r + k], ybuf.at[k], r, sem).start()
        return carry

    def wait(r, carry):
        for k in range(2):
            _row_copy(ys_ref, dest_ref[0, 2 * r + k], ybuf.at[k], r, sem).wait()
        return carry

    lax.fori_loop(0, tm, start, 0, unroll=8)
    lax.fori_loop(0, tm, wait, 0, unroll=8)
    sel = sel_ref[...]
    o_ref[...] = x_ref[...] + ybuf[0] * sel[:, 2:3] + ybuf[1] * sel[:, 3:4]


def _combine(dest_blk, ys, x, sel):
    n = x.shape[0]
    tm = TM_ROUTE
    return pl.pallas_call(
        functools.partial(_combine_kernel, tm=tm),
        grid=(n // tm,),
        in_specs=[pl.BlockSpec((None, 1, 2 * tm), lambda i: (i, 0, 0), memory_space=pltpu.SMEM),
                  pl.BlockSpec(memory_space=pl.ANY),
                  pl.BlockSpec((tm, D_MODEL), lambda i: (i, 0)),
                  pl.BlockSpec((tm, LANES), lambda i: (i, 0))],
        out_specs=pl.BlockSpec((tm, D_MODEL), lambda i: (i, 0)),
        out_shape=jax.ShapeDtypeStruct((n, D_MODEL), F32),
        scratch_shapes=[pltpu.VMEM((2, tm, D_MODEL), F32), pltpu.SemaphoreType.DMA(())],
        compiler_params=pltpu.CompilerParams(dimension_semantics=("arbitrary",), vmem_limit_bytes=VMEM_LIMIT),
        name="moe_combine",
    )(dest_blk, ys, x, sel)


def _moe_ffn(h, x, sel, lw):
    n = x.shape[0]
    tm = TM_MOE
    top_idx = sel[:, :2].astype(jnp.int32)
    flat_e = top_idx.reshape(-1)
    onehot = (flat_e[:, None] == jnp.arange(N_EXPERTS, dtype=jnp.int32)[None, :]).astype(jnp.int32)
    csum = jnp.cumsum(onehot, axis=0)
    rank = jnp.sum(csum * onehot, axis=1) - 1
    counts = csum[-1]
    padded = (counts + tm - 1) // tm * tm
    pend = jnp.cumsum(padded)
    pstart = pend - padded
    dest = pstart[flat_e] + rank
    p = n * 2 + N_EXPERTS * tm
    n_blk = p // tm
    blk_start = jnp.arange(n_blk, dtype=jnp.int32) * tm
    blk_e = jnp.minimum(jnp.sum((pend[None, :] <= blk_start[:, None]).astype(jnp.int32), axis=1), N_EXPERTS - 1)
    n_used = (pend[-1:] // tm).astype(jnp.int32)
    dest_blk = dest.astype(jnp.int32).reshape(n // TM_ROUTE, 1, 2 * TM_ROUTE)
    xs = _dispatch(dest_blk, h, p)
    ys = _moe_experts(xs, blk_e, n_used, lw)
    return _combine(dest_blk, ys, x, sel)


def _pad_cols(w, width):
    return jnp.pad(w, ((0, 0), (0, width - w.shape[1])))


def _head_groups(w, heads, dim):
    kdim = w.shape[0]
    return jnp.pad(w.reshape(kdim, heads, dim), ((0, 0), (0, 0), (0, LANES - dim))).reshape(kdim, heads * LANES)


def _layer_weights(layer, p):
    w_in = p["w_in"][layer]
    a, b, c = w_in[:, :COLS_A], w_in[:, COLS_A:COLS_A + COLS_B], w_in[:, COLS_A + COLS_B:]
    kpe = jnp.pad(a[:, MLA_Q_RANK + MLA_KV_RANK:], ((0, 0), (MLA_NOPE, LANES - MLA_QK)))
    bq = b[:, :WIDTH_B]

    def dup(w):
        return jnp.repeat(w.reshape(D_MODEL, SWA_KV_HEADS, 1, HEAD_DIM), 2, axis=2).reshape(D_MODEL, 2 * LANES)

    bk = dup(b[:, WIDTH_B:WIDTH_B + SWA_KV_HEADS * HEAD_DIM])
    bv = dup(b[:, WIDTH_B + SWA_KV_HEADS * HEAD_DIM:])
    w_in_p = jnp.concatenate([a[:, :MLA_Q_RANK + MLA_KV_RANK], kpe, bq, bk, bv, c], axis=1).astype(BF16)
    w_ukv = p["mla_w_ukv"][layer].reshape(MLA_KV_RANK, MLA_HEADS, MLA_NOPE + MLA_V)
    w_uk = _head_groups(w_ukv[:, :, :MLA_NOPE].reshape(MLA_KV_RANK, -1), MLA_HEADS, MLA_NOPE)
    w_uv = w_ukv[:, :, MLA_NOPE:].reshape(MLA_KV_RANK, -1)
    row = lambda g: g.reshape(1, -1).astype(F32)
    lw = {
        "attn_g": row(p["attn_norm_g"][layer]),
        "w_in": w_in_p,
        "cq_g": row(p["mla_cq_norm_g"][layer]),
        "w_uq": _head_groups(p["mla_w_uq"][layer], MLA_HEADS, MLA_QK).astype(BF16),
        "ckv_g": row(p["mla_ckv_norm_g"][layer]),
        "w_ukv": jnp.concatenate([w_uk, w_uv], axis=1).astype(BF16),
        "mla_qg": _pad_cols(row(p["mla_qn_g"][layer]), LANES),
        "mla_kg": _pad_cols(row(p["mla_kn_g"][layer]), LANES),
        "swa_qg": jnp.tile(row(p["swa_qn_g"][layer]), (1, 2)),
        "swa_kg": jnp.tile(row(p["swa_kn_g"][layer]), (1, 2)),
        "sinks": p["swa_sinks"][layer].astype(F32),
        "out_g": row(p["out_norm_g"][layer]),
        "w_out": p["w_out"][layer].astype(BF16),
        "ffn_g": row(p["ffn_norm_g"][layer]),
    }
    j = layer // 2
    if layer % 2 == 0:
        lw["w_gate"] = p["dense_w_gate"][j].astype(BF16)
        lw["w_up"] = p["dense_w_up"][j].astype(BF16)
        lw["w_down"] = p["dense_w_down"][j].astype(BF16)
    else:
        rw = _pad_cols(p["router_w"][j], LANES)
        hi = rw.astype(BF16)
        lw["router_hi"] = hi
        lw["router_lo"] = (rw - hi.astype(F32)).astype(BF16)
        lw["w_gate"] = p["moe_w_gate"][j].astype(BF16)
        lw["w_up"] = p["moe_w_up"][j].astype(BF16)
        lw["w_down"] = p["moe_w_down"][j].astype(BF16)
    return lw


def _rope_tables(seq):
    half = MLA_ROPE // 2
    inv = ROPE_THETA ** (-jnp.arange(half, dtype=F32) / half)
    ang = jnp.arange(seq).astype(F32)[:, None] * inv[None, :]
    cos, sin = jnp.cos(ang), jnp.sin(ang)
    zl = jnp.zeros((seq, MLA_NOPE), F32)
    zh = jnp.zeros((seq, half), F32)
    zr = jnp.zeros((seq, LANES - MLA_QK), F32)
    cos_t = jnp.concatenate([zl + 1.0, cos, cos, zr + 1.0], axis=1)
    sina = jnp.concatenate([zl, -sin, zh, zr], axis=1)
    sinb = jnp.concatenate([zl, zh, sin, zr], axis=1)
    return cos_t, sina, sinb


def kernel(x, attn_norm_g, w_in, mla_cq_norm_g, mla_w_uq, mla_ckv_norm_g, mla_w_ukv, mla_qn_g, mla_kn_g, swa_qn_g, swa_kn_g, swa_sinks, out_norm_g, w_out, ffn_norm_g, dense_w_gate, dense_w_up, dense_w_down, router_w, moe_w_gate, moe_w_up, moe_w_down):
    params = dict(attn_norm_g=attn_norm_g, w_in=w_in, mla_cq_norm_g=mla_cq_norm_g, mla_w_uq=mla_w_uq,
                  mla_ckv_norm_g=mla_ckv_norm_g, mla_w_ukv=mla_w_ukv, mla_qn_g=mla_qn_g, mla_kn_g=mla_kn_g,
                  swa_qn_g=swa_qn_g, swa_kn_g=swa_kn_g, swa_sinks=swa_sinks, out_norm_g=out_norm_g, w_out=w_out,
                  ffn_norm_g=ffn_norm_g, dense_w_gate=dense_w_gate, dense_w_up=dense_w_up,
                  dense_w_down=dense_w_down, router_w=router_w, moe_w_gate=moe_w_gate, moe_w_up=moe_w_up,
                  moe_w_down=moe_w_down)
    batch, seq, _ = x.shape
    depth = w_in.shape[0]
    tabs = _rope_tables(seq)
    slopes = jnp.exp2(-8.0 * jnp.arange(1, SWA_Q_HEADS + 1, dtype=F32) / SWA_Q_HEADS)
    tri = (jnp.arange(TC_SB)[:, None] > jnp.arange(TC_SB)[None, :]).astype(BF16)
    xf =x.reshape(batch * seq, D_MODEL)
    for layer in range(depth):
        lw = _layer_weights(layer, params)
        mq, mk, mv, sq, sk, sv, cq, ck, cv = _in_proj(xf, lw, tabs, seq)
        o_a = _mla_attention(mq, mk, mv, batch, seq)
        o_b = _swa_attention(sq, sk, sv, slopes, lw["sinks"], batch, seq)
        o_c = _sb_attention(cq, ck, cv, tri, batch, seq)
        if layer % 2 == 0:
            xf, h = _out_proj(o_a, o_b, o_c, xf, lw, moe=False)
            xf = _dense_ffn(h, xf, lw)
        else:
            xf, h, sel = _out_proj(o_a, o_b, o_c, xf, lw, moe=True)
            xf = _moe_ffn(h, xf, sel, lw)
    return xf.reshape(batch, seq, D_MODEL)
```
